```python
import math
import jax
import jax.numpy as jnp
from jax import lax
import numpy as np

D_MODEL = 2048
BATCH = 2
SEQ = 8192
DEPTH = 4

N_META = 16
HEAD_DIM = 64
SSM_WIDTH = D_MODEL // 2
SSM_GROUP = 16
SSM_GROUPS = SSM_WIDTH // SSM_GROUP
SSM_STATE = 64
FOX_WIDTH = D_MODEL // 4
FOX_HEADS = FOX_WIDTH // HEAD_DIM
DSA_WIDTH = D_MODEL - SSM_WIDTH - FOX_WIDTH
DSA_HEADS = DSA_WIDTH // HEAD_DIM
KV_RANK = D_MODEL // 16
IDX_HEADS = 16
IDX_DIM = 64
TOPK_MAX = 256
ROPE_THETA = 500000.0
ROT_DIM = HEAD_DIM // 4
D_FF = 11 * D_MODEL // 4
CONV_WIDTH = 3
Q_BLOCK = 128
LN_EPS = 1e-5
RMS_EPS = 1e-6
NEG_INF = -1e30
DT_MIN = 0.001
DT_MAX = 0.1
ALPHA = (2.0 * DEPTH) ** 0.25
BETA = (8.0 * DEPTH) ** -0.25

IN_SPLITS = (SSM_WIDTH, FOX_WIDTH, FOX_WIDTH, FOX_WIDTH, FOX_HEADS, DSA_WIDTH, KV_RANK,
             IDX_HEADS * IDX_DIM, IDX_DIM, IDX_HEADS)
IN_WIDTH = sum(IN_SPLITS)
SPLIT_POINTS = tuple(int(s) for s in np.cumsum(IN_SPLITS)[:-1])
GROUP_SPLIT_POINTS = (SSM_WIDTH, SSM_WIDTH + FOX_WIDTH)

kernel_name = 'hybrid_s5_fox_dsa_trunk'


def _layer_norm(x, g, b):
    xf = x.astype(jnp.float32)
    mu = jnp.mean(xf, axis=-1, keepdims=True)
    var = jnp.mean(jnp.square(xf - mu), axis=-1, keepdims=True)
    y = (xf - mu) * lax.rsqrt(var + LN_EPS) * g.astype(jnp.float32) + b.astype(jnp.float32)
    return y.astype(x.dtype)


def _rms_norm(x, g):
    xf = x.astype(jnp.float32)
    y = xf * lax.rsqrt(jnp.mean(jnp.square(xf), axis=-1, keepdims=True) + RMS_EPS) * g.astype(jnp.float32)
    return y.astype(x.dtype)


def _rope_tables(n):
    pos = jnp.arange(n, dtype=jnp.float32)
    inv_freq = ROPE_THETA ** (-jnp.arange(0, ROT_DIM, 2, dtype=jnp.float32) / ROT_DIM)
    ang = pos[:, None] * inv_freq[None, :]
    return jnp.cos(ang), jnp.sin(ang)


def _partial_rope(x, cos, sin):
    half = ROT_DIM // 2
    xr = x[..., :ROT_DIM].astype(jnp.float32)
    x1, x2 = xr[..., :half], xr[..., half:]
    c = cos[None, :, None, :]
    s = sin[None, :, None, :]
    rot = jnp.concatenate([x1 * c - x2 * s, x2 * c + x1 * s], axis=-1).astype(x.dtype)
    return jnp.concatenate([rot, x[..., ROT_DIM:]], axis=-1)


def _to_query_blocks(a, pad):
    a = jnp.pad(a, [(0, 0), (pad, 0)] + [(0, 0)] * (a.ndim - 2))
    a = a.reshape(a.shape[0], a.shape[1] // Q_BLOCK, Q_BLOCK, *a.shape[2:])
    return jnp.moveaxis(a, 1, 0)


def _from_query_blocks(o, pad):
    o = jnp.moveaxis(o, 0, 1)
    o = o.reshape(o.shape[0], o.shape[1] * o.shape[2], *o.shape[3:])
    return o[:, pad:]


def _query_positions(n, pad):
    kpos = jnp.arange(n, dtype=jnp.int32)
    qpos = jnp.concatenate([jnp.zeros((pad,), jnp.int32), kpos]).reshape(-1, Q_BLOCK)
    return kpos, qpos


def _complex_affine_combine(e1, e2):
    a1r, a1i, b1r, b1i = e1
    a2r, a2i, b2r, b2i = e2
    return (a1r * a2r - a1i * a2i,
            a1r * a2i + a1i * a2r,
            a2r * b1r - a2i * b1i + b2r,
            a2r * b1i + a2i * b1r + b2i)


def _s5_mixer(u, lam_re, lam_im, log_dt, b_re, b_im, c_re, c_im, d_skip, glu_w, glu_b):
    f32 = jnp.float32
    bsz, n, _ = u.shape
    uf = u.astype(f32)
    ug = uf.reshape(bsz, n, SSM_GROUPS, SSM_GROUP)
    lr, li = lam_re.astype(f32), lam_im.astype(f32)
    dt = jnp.exp(log_dt.astype(f32))[:, None]
    mag = jnp.exp(lr * dt)
    ab_re, ab_im = mag * jnp.cos(li * dt), mag * jnp.sin(li * dt)
    den = lr * lr + li * li
    nr, ni = ab_re - 1.0, ab_im
    zr = (nr * lr + ni * li) / den
    zi = (ni * lr - nr * li) / den
    br, bi = b_re.astype(f32), b_im.astype(f32)
    bb_re = zr[..., None] * br - zi[..., None] * bi
    bb_im = zr[..., None] * bi + zi[..., None] * br
    bu_re = jnp.einsum('gpc,btgc->btgp', bb_re, ug)
    bu_im = jnp.einsum('gpc,btgc->btgp', bb_im, ug)
    a_re = jnp.broadcast_to(ab_re, (1, n) + ab_re.shape)
    a_im = jnp.broadcast_to(ab_im, (1, n) + ab_im.shape)
    _, _, s_re, s_im = lax.associative_scan(_complex_affine_combine, (a_re, a_im, bu_re, bu_im), axis=1)
    y = (jnp.einsum('gcp,btgp->btgc', c_re.astype(f32), s_re)
         - jnp.einsum('gcp,btgp->btgc', c_im.astype(f32), s_im))
    y = y.reshape(bsz, n, SSM_WIDTH) + d_skip.astype(f32) * uf
    g = jax.nn.gelu(y)
    out = g * jax.nn.sigmoid(g @ glu_w.astype(f32) + glu_b.astype(f32))
    return out.astype(u.dtype)


def _fox_attention(q, k, v, f_logit, f_bias):
    f32 = jnp.float32
    bsz, n, h, dh = q.shape
    log_f = jax.nn.log_sigmoid(f_logit.astype(f32) + f_bias.astype(f32))
    cum = jnp.cumsum(log_f, axis=1)
    cum_k = jnp.transpose(cum, (0, 2, 1))
    pad = (-n) % Q_BLOCK
    kpos, qpos = _query_positions(n, pad)
    scale = dh ** -0.5

    def block(args):
        qb, cq, pq = args
        s = jnp.einsum('bqhd,bkhd->bhqk', qb, k, preferred_element_type=f32) * scale
        s = s + jnp.transpose(cq, (0, 2, 1))[..., None] - cum_k[:, :, None, :]
        s = jnp.where((kpos[None, :] <= pq[:, None])[None, None], s, NEG_INF)
        p = jax.nn.softmax(s, axis=-1)
        return jnp.einsum('bhqk,bkhd->bqhd', p.astype(v.dtype), v)

    out = lax.map(block, (_to_query_blocks(q, pad), _to_query_blocks(cum, pad), qpos))
    return _from_query_blocks(out, pad)


def _dsa_attention(q, k, v, iq, ik, iw, topk):
    f32 = jnp.float32
    bsz, n, h, dh = q.shape
    k_flat = k.reshape(bsz, n, h * dh)
    v_flat = v.reshape(bsz, n, h * dh)
    ikf = ik.astype(f32)
    pad = (-n) % Q_BLOCK
    kpos, qpos = _query_positions(n, pad)
    scale = dh ** -0.5
    bidx = jnp.arange(bsz)[:, None]

    def block(args):
        qb, iqb, iwb, pq = args
        logits = jnp.einsum('bqjd,bkd->bqjk', iqb.astype(f32), ikf) * (IDX_DIM ** -0.5)
        score = jnp.einsum('bqjk,bqj->bqk', jax.nn.relu(logits), iwb.astype(f32) * (IDX_HEADS ** -0.5))
        score = jnp.where((kpos[None, :] <= pq[:, None])[None], score, NEG_INF)
        _, idx = lax.top_k(score, topk)
        flat = idx.reshape(bsz, Q_BLOCK * topk)
        kg = k_flat[bidx, flat].reshape(bsz, Q_BLOCK, topk, h, dh)
        vg = v_flat[bidx, flat].reshape(bsz, Q_BLOCK, topk, h, dh)
        s = jnp.einsum('bqhd,bqkhd->bhqk', qb, kg, preferred_element_type=f32) * scale
        valid = (idx <= pq[None, :, None])[:, None]
        s = jnp.where(valid, s, NEG_INF)
        p = jax.nn.softmax(s, axis=-1)
        return jnp.einsum('bhqk,bqkhd->bqhd', p.astype(vg.dtype), vg)

    out = lax.map(block, (_to_query_blocks(q, pad), _to_query_blocks(iq, pad),
                          _to_query_blocks(iw, pad), qpos))
    return _from_query_blocks(out, pad)


def _mixer(h, cos, sin, topk, w_in, lam_re, lam_im, log_dt, b_re, b_im, c_re, c_im, d_skip,
           glu_w, glu_b, f_bias, kv_norm_g, w_kv_up, mix_norm_g, w_out):
    bsz, n, _ = h.shape
    proj = h @ w_in
    u, fq, fk, fv, ff, dq, ckv, iq, ik, iw = jnp.split(proj, SPLIT_POINTS, axis=-1)
    y_ssm = _s5_mixer(u, lam_re, lam_im, log_dt, b_re, b_im, c_re, c_im, d_skip, glu_w, glu_b)
    y_fox = _fox_attention(fq.reshape(bsz, n, FOX_HEADS, HEAD_DIM), fk.reshape(bsz, n, FOX_HEADS, HEAD_DIM),
                           fv.reshape(bsz, n, FOX_HEADS, HEAD_DIM), ff, f_bias)
    kv = _rms_norm(ckv, kv_norm_g) @ w_kv_up
    dk, dv = jnp.split(kv, 2, axis=-1)
    dq = _partial_rope(dq.reshape(bsz, n, DSA_HEADS, HEAD_DIM), cos, sin)
    dk = _partial_rope(dk.reshape(bsz, n, DSA_HEADS, HEAD_DIM), cos, sin)
    dv = dv.reshape(bsz, n, DSA_HEADS, HEAD_DIM)
    iq = _partial_rope(iq.reshape(bsz, n, IDX_HEADS, IDX_DIM), cos, sin)
    ik = _partial_rope(ik[:, :, None, :], cos, sin)[:, :, 0, :]
    y_dsa = _dsa_attention(dq, dk, dv, iq, ik, iw, topk)
    g_ssm, g_fox, g_dsa = jnp.split(mix_norm_g, GROUP_SPLIT_POINTS)
    y = jnp.concatenate([_rms_norm(y_ssm, g_ssm),
                         _rms_norm(y_fox.reshape(bsz, n, FOX_WIDTH), g_fox),
                         _rms_norm(y_dsa.reshape(bsz, n, DSA_WIDTH), g_dsa)], axis=-1)
    return y @ w_out


def _conv_ffn(h, w_up, conv_w, conv_b, w_down):
    a = h @ w_up
    c = a.shape[-1]
    a = lax.conv_general_dilated(a, conv_w[:, None, :].astype(a.dtype), window_strides=(1,),
                                 padding=[(CONV_WIDTH - 1, 0)],
                                 dimension_numbers=('NWC', 'WIO', 'NWC'),
                                 feature_group_count=c) + conv_b
    val, gate = jnp.split(a, 2, axis=-1)
    return (jax.nn.silu(gate) * val) @ w_down


def setup_inputs(seed: int = 0) -> dict:
    key = jax.random.key(seed)
    ks = jax.random.split(key, 32)
    f32 = jnp.float32
    L = DEPTH

    def nrm(k, shape, scale):
        return jax.random.normal(k, shape, f32) * scale

    x = nrm(ks[0], (BATCH, SEQ, D_MODEL), 1.0)
    meta_tokens = nrm(ks[1], (N_META, D_MODEL), 1.0)
    ln_in_g = 1.0 + nrm(ks[2], (D_MODEL,), 0.02)
    ln_in_b = nrm(ks[3], (D_MODEL,), 0.02)
    w_in = nrm(ks[4], (L, D_MODEL, IN_WIDTH), D_MODEL ** -0.5)
    n_idx = jnp.arange(SSM_STATE, dtype=f32)
    ssm_lam_re = -0.5 + nrm(ks[5], (L, SSM_GROUPS, SSM_STATE), 0.01)
    ssm_lam_im = jnp.pi * n_idx + nrm(ks[6], (L, SSM_GROUPS, SSM_STATE), 0.01)
    ssm_log_dt = jax.random.uniform(ks[7], (L, SSM_GROUPS), f32, math.log(DT_MIN), math.log(DT_MAX))
    b_scale = (2.0 * SSM_GROUP) ** -0.5
    ssm_b_re = nrm(ks[8], (L, SSM_GROUPS, SSM_STATE, SSM_GROUP), b_scale)
    ssm_b_im = nrm(ks[9], (L, SSM_GROUPS, SSM_STATE, SSM_GROUP), b_scale)
    c_scale = SSM_STATE ** -0.5
    ssm_c_re = nrm(ks[10], (L, SSM_GROUPS, SSM_GROUP, SSM_STATE), c_scale)
    ssm_c_im = nrm(ks[11], (L, SSM_GROUPS, SSM_GROUP, SSM_STATE), c_scale)
    ssm_d = nrm(ks[12], (L, SSM_WIDTH), 1.0)
    ssm_glu_w = nrm(ks[13], (L, SSM_WIDTH, SSM_WIDTH), SSM_WIDTH ** -0.5)
    ssm_glu_b = nrm(ks[14], (L, SSM_WIDTH), 0.02)
    fox_f_bias = jax.random.uniform(ks[15], (L, FOX_HEADS), f32, 2.0, 5.0)
    kv_norm_g = 1.0 + nrm(ks[16], (L, KV_RANK), 0.02)
    w_kv_up = nrm(ks[17], (L, KV_RANK, 2 * DSA_WIDTH), KV_RANK ** -0.5)
    mix_norm_g = 1.0 + nrm(ks[18], (L, D_MODEL), 0.02)
    w_out = nrm(ks[19], (L, D_MODEL, D_MODEL), BETA * D_MODEL ** -0.5)
    ln1_g = 1.0 + nrm(ks[20], (L, D_MODEL), 0.02)
    ln1_b = nrm(ks[21], (L, D_MODEL), 0.02)
    w_up = nrm(ks[22], (L, D_MODEL, 2 * D_FF), D_MODEL ** -0.5)
    conv_w = nrm(ks[23], (L, CONV_WIDTH, 2 * D_FF), CONV_WIDTH ** -0.5)
    conv_b = nrm(ks[24], (L, 2 * D_FF), 0.02)
    w_down = nrm(ks[25], (L, D_FF, D_MODEL), BETA * D_FF ** -0.5)
    ln2_g = 1.0 + nrm(ks[26], (L, D_MODEL), 0.02)
    ln2_b = nrm(ks[27], (L, D_MODEL), 0.02)
    return {'x': x, 'meta_tokens': meta_tokens, 'ln_in_g': ln_in_g, 'ln_in_b': ln_in_b, 'w_in': w_in,
            'ssm_lam_re': ssm_lam_re, 'ssm_lam_im': ssm_lam_im, 'ssm_log_dt': ssm_log_dt,
            'ssm_b_re': ssm_b_re, 'ssm_b_im': ssm_b_im, 'ssm_c_re': ssm_c_re, 'ssm_c_im': ssm_c_im,
            'ssm_d': ssm_d, 'ssm_glu_w': ssm_glu_w, 'ssm_glu_b': ssm_glu_b, 'fox_f_bias': fox_f_bias,
            'kv_norm_g': kv_norm_g, 'w_kv_up': w_kv_up, 'mix_norm_g': mix_norm_g, 'w_out': w_out,
            'ln1_g': ln1_g, 'ln1_b': ln1_b, 'w_up': w_up, 'conv_w': conv_w, 'conv_b': conv_b,
            'w_down': w_down, 'ln2_g': ln2_g, 'ln2_b': ln2_b}


def reference(x, meta_tokens, ln_in_g, ln_in_b, w_in, ssm_lam_re, ssm_lam_im, ssm_log_dt,
              ssm_b_re, ssm_b_im, ssm_c_re, ssm_c_im, ssm_d, ssm_glu_w, ssm_glu_b, fox_f_bias,
              kv_norm_g, w_kv_up, mix_norm_g, w_out, ln1_g, ln1_b, w_up, conv_w, conv_b,
              w_down, ln2_g, ln2_b):
    bsz, seq, _ = x.shape
    topk = min(TOPK_MAX, seq // 4)
    meta = jnp.broadcast_to(meta_tokens[None].astype(x.dtype), (bsz, N_META, D_MODEL))
    h = _layer_norm(jnp.concatenate([meta, x], axis=1), ln_in_g, ln_in_b)
    cos, sin = _rope_tables(seq + N_META)
    for l in range(DEPTH):
        m = _mixer(h, cos, sin, topk, w_in[l], ssm_lam_re[l], ssm_lam_im[l], ssm_log_dt[l],
                   ssm_b_re[l], ssm_b_im[l], ssm_c_re[l], ssm_c_im[l], ssm_d[l], ssm_glu_w[l],
                   ssm_glu_b[l], fox_f_bias[l], kv_norm_g[l], w_kv_up[l], mix_norm_g[l], w_out[l])
        h = _layer_norm(ALPHA * h + m, ln1_g[l], ln1_b[l])
        f = _conv_ffn(h, w_up[l], conv_w[l], conv_b[l], w_down[l])
        h = _layer_norm(ALPHA * h + f, ln2_g[l], ln2_b[l])
    return h[:, N_META:, :]
```

```python
import functools
import math

import jax
import jax.numpy as jnp
import numpy as np
from jax import lax
from jax.experimental import pallas as pl
from jax.experimental.pallas import tpu as pltpu

F32 = jnp.float32
BF16 = jnp.bfloat16
I32 = jnp.int32

N_META = 16
HEAD_DIM = 64
SSM_GROUP = 16
SSM_STATE = 64
IDX_HEADS = 16
IDX_DIM = 64
TOPK_MAX = 256
ROPE_THETA = 500000.0
ROT_DIM = HEAD_DIM // 4
CONV_WIDTH = 3
LN_EPS = 1e-5
RMS_EPS = 1e-6
NEG_INF = -1e30

LANES = 128
SUBLANES = 8
VMEM_LIMIT_BYTES = 56 * 1024 * 1024

SEQ_BLOCK = 256
AUG_ROWS = 80
INT_MIN = -(2 ** 31)


def _pick(n, candidates):
    for c in candidates:
        if n % c == 0:
            return c
    raise ValueError(f"no tile for {n} in {candidates}")


def _cparams(sem, vmem=VMEM_LIMIT_BYTES):
    return pltpu.CompilerParams(dimension_semantics=sem, vmem_limit_bytes=vmem)


def _resident(shape, index_map):
    return pl.BlockSpec(shape, index_map, pipeline_mode=pl.Buffered(1))


def _ln_rows(z, g, b):
    mu = jnp.mean(z, axis=-1, keepdims=True)
    zc = z - mu
    var = jnp.mean(zc * zc, axis=-1, keepdims=True)
    return zc * lax.rsqrt(var + LN_EPS) * g + b


def _ln_kernel(x_ref, g_ref, b_ref, o32_ref, o16_ref):
    y = _ln_rows(x_ref[...], g_ref[...], b_ref[...])
    o32_ref[...] = y
    o16_ref[...] = y.astype(BF16)


def _add_ln_kernel(alpha, h_ref, f_ref, g_ref, b_ref, o32_ref, o16_ref):
    y = _ln_rows(alpha * h_ref[...] + f_ref[...], g_ref[...], b_ref[...])
    o32_ref[...] = y
    o16_ref[...] = y.astype(BF16)


def _layer_norm(x, g, b, f=None, alpha=1.0):
    m, d = x.shape
    tm = _pick(m, (512, 256, 128))
    row = pl.BlockSpec((tm, d), lambda i: (i, 0))
    vec = pl.BlockSpec((1, d), lambda i: (0, 0))
    outs = (jax.ShapeDtypeStruct((m, d), F32), jax.ShapeDtypeStruct((m, d), BF16))
    if f is None:
        return pl.pallas_call(
            _ln_kernel, grid=(m // tm,), in_specs=[row, vec, vec], out_specs=(row, row),
            out_shape=outs, compiler_params=_cparams(("parallel",)), name="layer_norm",
        )(x, g.reshape(1, d), b.reshape(1, d))
    return pl.pallas_call(
        functools.partial(_add_ln_kernel, alpha), grid=(m // tm,), in_specs=[row, row, vec, vec],
        out_specs=(row, row), out_shape=outs, compiler_params=_cparams(("parallel",)),
        name="residual_layer_norm",
    )(x, f, g.reshape(1, d), b.reshape(1, d))


def _mm_kernel(a_ref, w_ref, o_ref):
    o_ref[...] = jnp.dot(a_ref[...], w_ref[...], preferred_element_type=F32).astype(o_ref.dtype)


def _matmul(a, w, out_dtype):
    m, k = a.shape
    n = w.shape[1]
    tm = _pick(m, (1536, 1024, 768, 512, 256))
    tn = _pick(n, (1408, 1024, 512, 256, 128))
    return pl.pallas_call(
        _mm_kernel, grid=(m // tm, n // tn),
        in_specs=[pl.BlockSpec((tm, k), lambda i, j: (i, 0)), pl.BlockSpec((k, tn), lambda i, j: (0, j))],
        out_specs=pl.BlockSpec((tm, tn), lambda i, j: (i, j)),
        out_shape=jax.ShapeDtypeStruct((m, n), out_dtype),
        compiler_params=_cparams(("parallel", "parallel")), name="matmul",
    )(a, w)


def _rope_perm():
    p = np.zeros((LANES, LANES), np.float32)
    half = ROT_DIM // 2
    for c in range(LANES):
        cc = c % HEAD_DIM
        if cc < half:
            p[c + half, c] = -1.0
        elif cc < ROT_DIM:
            p[c - half, c] = 1.0
    return jnp.asarray(p, BF16)


def _rope_tables(tp):
    pos = jnp.arange(tp, dtype=F32)
    inv_freq = ROPE_THETA ** (-jnp.arange(0, ROT_DIM, 2, dtype=F32) / ROT_DIM)
    ang = pos[:, None] * inv_freq[None, :]
    cos, sin = jnp.cos(ang), jnp.sin(ang)
    half = ROT_DIM // 2
    lane = np.arange(LANES) % HEAD_DIM
    rot = jnp.asarray(lane < ROT_DIM)
    idx = jnp.asarray(lane % half)
    cos_t = jnp.where(rot[None, :], cos[:, idx], 1.0)
    sin_t = jnp.where(rot[None, :], sin[:, idx], 0.0)
    return cos_t, sin_t


def _split3(x):
    hi = x.astype(BF16)
    r1 = x - hi.astype(F32)
    mid = r1.astype(BF16)
    lo = (r1 - mid.astype(F32)).astype(BF16)
    return hi, mid, lo


def _prep_kernel(dq_ref, iq_ref, ckv_ref, misc_ref, cos_ref, sin_ref, perm_ref, kvg_ref, wkv_ref, fb_ref,
                 dq_o, dk_o, dv_o, iq_o, misc_o, cum_o, carry_ref):
    t = pl.program_id(1)
    cos = cos_ref[...]
    sin = sin_ref[...]
    perm = perm_ref[...]

    def rope(x):
        outs = []
        for j in range(x.shape[1] // LANES):
            xt = x[:, j * LANES:(j + 1) * LANES]
            yt = jnp.dot(xt.astype(BF16), perm, preferred_element_type=F32)
            outs.append(xt * cos + yt * sin)
        return outs[0] if len(outs) == 1 else jnp.concatenate(outs, axis=1)

    scale = HEAD_DIM ** -0.5
    dq_o[...] = (rope(dq_ref[...].astype(F32)) * scale).astype(BF16)
    iq_o[...] = (rope(iq_ref[...].astype(F32)) * (IDX_DIM ** -0.5)).astype(BF16)

    ckv = ckv_ref[...].astype(F32)
    ckvn = ckv * lax.rsqrt(jnp.mean(ckv * ckv, axis=-1, keepdims=True) + RMS_EPS) * kvg_ref[...]
    kv = jnp.dot(ckvn.astype(BF16), wkv_ref[...], preferred_element_type=F32)
    half = kv.shape[1] // 2
    dk_o[...] = rope(kv[:, :half]).astype(BF16)
    dv_o[...] = kv[:, half:].astype(BF16)

    x = misc_ref[...]
    lane = lax.broadcasted_iota(I32, x.shape, 1)
    logf = jax.nn.log_sigmoid(x + fb_ref[...])
    logf = jnp.where((lane >= IDX_DIM) & (lane < IDX_DIM + 8), logf, 0.0)
    misc_o[...] = jnp.where(lane < IDX_DIM, rope(x), x * (IDX_HEADS ** -0.5))

    @pl.when(t == 0)
    def _():
        carry_ref[...] = jnp.zeros_like(carry_ref)

    tb = x.shape[0]
    r = lax.broadcasted_iota(I32, (tb, tb), 0)
    c = lax.broadcasted_iota(I32, (tb, tb), 1)
    tril = jnp.where(r >= c, 1.0, 0.0).astype(BF16)
    hi, mid, lo = _split3(logf)
    cum = (jnp.dot(tril, hi, preferred_element_type=F32) + jnp.dot(tril, mid, preferred_element_type=F32)
           + jnp.dot(tril, lo, preferred_element_type=F32)) + carry_ref[0:1, :]
    cum_o[...] = cum
    carry_ref[0:1, :] = cum[tb - 1:tb, :]


def _prep(proj_a, proj_b, cos_t, sin_t, kv_norm_g, w_kv_up, f_bias, bsz, tp):
    tb = SEQ_BLOCK
    na = proj_a.shape[-1]
    pa = proj_a.reshape(bsz, tp, na)
    pb = proj_b.reshape(bsz, tp, LANES)
    fb = jnp.zeros((1, LANES), F32).at[0, IDX_DIM:IDX_DIM + 8].set(f_bias)
    bs = lambda w, cb: pl.BlockSpec((None, tb, w), lambda b, t: (b, t, cb))
    tab = pl.BlockSpec((tb, LANES), lambda b, t: (t, 0))
    const = lambda shp: pl.BlockSpec(shp, lambda b, t: (0,) * len(shp))
    out_shapes = (
        jax.ShapeDtypeStruct((bsz, tp, 512), BF16), jax.ShapeDtypeStruct((bsz, tp, 512), BF16),
        jax.ShapeDtypeStruct((bsz, tp, 512), BF16), jax.ShapeDtypeStruct((bsz, tp, 1024), BF16),
        jax.ShapeDtypeStruct((bsz, tp, LANES), F32), jax.ShapeDtypeStruct((bsz, tp, LANES), F32),
    )
    return pl.pallas_call(
        _prep_kernel, grid=(bsz, tp // tb),
        in_specs=[bs(512, 5), bs(1024, 3), bs(LANES, 32), bs(LANES, 0), tab, tab, const((LANES, LANES)),
                  const((1, LANES)), const((LANES, 1024)), const((1, LANES))],
        out_specs=(bs(512, 0), bs(512, 0), bs(512, 0), bs(1024, 0), bs(LANES, 0), bs(LANES, 0)),
        out_shape=out_shapes, scratch_shapes=[pltpu.VMEM((SUBLANES, LANES), F32)],
        compiler_params=_cparams(("arbitrary", "arbitrary")), name="prep",
    )(pa, pa, pa, pb, cos_t, sin_t, _rope_perm(), kv_norm_g.reshape(1, LANES), w_kv_up.astype(BF16), fb)


_NT = (((1,), (1,)), ((), ()))


def _attend(q, k_at, v_at, n_full, diag_mask, sel_at):
    tq = q.shape[0]

    def step(kb, carry, masked):
        m, acc = carry
        s = lax.dot_general(k_at(kb), q, _NT, preferred_element_type=F32)
        keep = sel_at(kb)
        if masked:
            keep = diag_mask if keep is None else (keep & diag_mask)
        if keep is not None:
            s = jnp.where(keep, s, NEG_INF)
        m_new = jnp.maximum(m, jnp.max(s, axis=0, keepdims=True))
        p = jnp.exp(s - m_new)
        acc = acc * jnp.exp(m - m_new) + jnp.dot(v_at(kb), p.astype(BF16), preferred_element_type=F32)
        return m_new, acc

    init = (jnp.full((1, tq), NEG_INF, F32), jnp.zeros((AUG_ROWS, tq), F32))
    carry = lax.fori_loop(0, n_full, lambda kb, c: step(kb, c, False), init)
    _, acc = step(n_full, carry, True)
    return acc[:HEAD_DIM, :] / acc[HEAD_DIM:HEAD_DIM + 1, :]


def _group_norm_t(y_t, gain_t):
    ms = jnp.mean(y_t * y_t, axis=0, keepdims=True)
    return (y_t * lax.rsqrt(ms + RMS_EPS) * gain_t).T.astype(BF16)


def _fox_kernel(q_ref, k_ref, v_ref, g_ref, o_ref):
    i = pl.program_id(1)
    tq = q_ref.shape[1]
    r = lax.broadcasted_iota(I32, (tq, tq), 0)
    c = lax.broadcasted_iota(I32, (tq, tq), 1)
    diag = r <= c
    outs = []
    for h in range(q_ref.shape[0]):
        k_at = lambda kb, h=h: k_ref[h, pl.ds(pl.multiple_of(kb * tq, tq), tq), :]
        v_at = lambda kb, h=h: v_ref[h, :, pl.ds(pl.multiple_of(kb * tq, tq), tq)]
        outs.append(_attend(q_ref[h], k_at, v_at, i, diag, lambda kb: None))
    o_ref[...] = _group_norm_t(jnp.concatenate(outs, axis=0), g_ref[...])


def _fox_attention(q_aug, k_aug, v_t, gain_t, bsz, tp):
    tq = SEQ_BLOCK
    nh = q_aug.shape[1]
    return pl.pallas_call(
        _fox_kernel, grid=(bsz, tp // tq),
        in_specs=[pl.BlockSpec((None, nh, tq, LANES), lambda b, i: (b, 0, i, 0)),
                  _resident((None, nh, tp, LANES), lambda b, i: (b, 0, 0, 0)),
                  _resident((None, nh, AUG_ROWS, tp), lambda b, i: (b, 0, 0, 0)),
                  _resident((nh * HEAD_DIM, tq), lambda b, i: (0, 0))],
        out_specs=pl.BlockSpec((None, tq, nh * HEAD_DIM), lambda b, i: (b, i, 0)),
        out_shape=jax.ShapeDtypeStruct((bsz, tp, nh * HEAD_DIM), BF16),
        compiler_params=_cparams(("arbitrary", "arbitrary")), name="fox_attention",
    )(q_aug, k_aug, v_t, gain_t)


def _order_key(x):
    b = lax.bitcast_convert_type(x + 0.0, I32)
    return b ^ ((b >> 31) & 0x7FFFFFFF)


def _dsa_kernel(topk, iq_ref, iw_ref, q_ref, ik_ref, k_ref, v_ref, g_ref, o_ref, key_ref):
    i = pl.program_id(1)
    tq = q_ref.shape[1]
    tk = tq
    r = lax.broadcasted_iota(I32, (tk, tq), 0)
    c = lax.broadcasted_iota(I32, (tk, tq), 1)
    diag = r <= c
    blk = lambda kb: pl.ds(pl.multiple_of(kb * tk, tk), tk)
    heads_per_dot = 4

    def score_block(kb, masked):
        ik = ik_ref[blk(kb), :]
        sc = jnp.zeros((tk, tq), F32)
        for j0 in range(0, IDX_HEADS, heads_per_dot):
            lg = lax.dot_general(ik, iq_ref[j0 * tq:(j0 + heads_per_dot) * tq, :], _NT,
                                 preferred_element_type=F32)
            for j in range(heads_per_dot):
                sc = sc + jnp.maximum(lg[:, j * tq:(j + 1) * tq], 0.0) * iw_ref[j0 + j:j0 + j + 1, :]
        if masked:
            sc = jnp.where(diag, sc, NEG_INF)
        key_ref[blk(kb), :] = _order_key(sc)

    def score_loop(kb, carry):
        score_block(kb, False)
        return carry

    lax.fori_loop(0, i, score_loop, 0)
    score_block(i, True)

    def count_ge(cand):
        def body(kb, cnt):
            hit = jnp.where(key_ref[blk(kb), :] >= cand, 1, 0).reshape(tk // SUBLANES, SUBLANES, tq)
            return cnt + jnp.sum(hit, axis=0)
        cnt = lax.fori_loop(0, i + 1, body, jnp.zeros((SUBLANES, tq), I32))
        return jnp.sum(cnt, axis=0, keepdims=True)

    t_u = jnp.zeros((1, tq), I32)
    for bit in range(31, -1, -1):
        cand_u = t_u | np.uint32(1 << bit).view(np.int32)
        t_u = jnp.where(count_ge(cand_u ^ np.int32(INT_MIN)) >= topk, cand_u, t_u)
    thr = t_u ^ np.int32(INT_MIN)
    floor_key = int(np.array(NEG_INF, np.float32).view(np.int32)) ^ 0x7FFFFFFF
    thr = jnp.maximum(thr, np.int32(floor_key + 1))

    outs = []
    for h in range(q_ref.shape[0]):
        k_at = lambda kb, h=h: k_ref[h // 2, blk(kb), :]
        v_at = lambda kb, h=h: v_ref[h, :, blk(kb)]
        sel_at = lambda kb: key_ref[blk(kb), :] >= thr
        outs.append(_attend(q_ref[h], k_at, v_at, i, diag, sel_at))
    o_ref[...] = _group_norm_t(jnp.concatenate(outs, axis=0), g_ref[...])


def _dsa_attention(iq_hm, iw_t, q_pad, ik, k_pair, v_t, gain_t, topk, bsz, tp):
    tq = SEQ_BLOCK
    nh = q_pad.shape[1]
    return pl.pallas_call(
        functools.partial(_dsa_kernel, topk), grid=(bsz, tp // tq),
        in_specs=[pl.BlockSpec((None, None, IDX_HEADS * tq, IDX_DIM), lambda b, i: (b, i, 0, 0)),
                  pl.BlockSpec((None, IDX_HEADS, tq), lambda b, i: (b, 0, i)),
                  pl.BlockSpec((None, nh, tq, LANES), lambda b, i: (b, 0, i, 0)),
                  _resident((None, tp, IDX_DIM), lambda b, i: (b, 0, 0)),
                  _resident((None, nh // 2, tp, LANES), lambda b, i: (b, 0, 0, 0)),
                  _resident((None, nh, AUG_ROWS, tp), lambda b, i: (b, 0, 0, 0)),
                  _resident((nh * HEAD_DIM, tq), lambda b, i: (0, 0))],
        out_specs=pl.BlockSpec((None, tq, nh * HEAD_DIM), lambda b, i: (b, i, 0)),
        out_shape=jax.ShapeDtypeStruct((bsz, tp, nh * HEAD_DIM), BF16),
        scratch_shapes=[pltpu.VMEM((tp, tq), I32)],
        compiler_params=_cparams(("arbitrary", "arbitrary")), name="dsa_attention",
    )(iq_hm, iw_t, q_pad, ik, k_pair, v_t, gain_t)


def _s5_kernel(u_ref, bre_ref, bim_ref, are_ref, aim_ref, cre_ref, cim_ref, d_ref, gw_ref, gb_ref, gn_ref,
               o_ref, sre_ref, sim_ref, cre_carry, cim_carry):
    t = pl.program_id(1)
    tb = u_ref.shape[0]
    ntile = bre_ref.shape[0]
    sw = bre_ref.shape[2]

    @pl.when(t == 0)
    def _():
        cre_carry[...] = jnp.zeros_like(cre_carry)
        cim_carry[...] = jnp.zeros_like(cim_carry)

    u = u_ref[...]
    for l in range(ntile):
        ul = u[:, l * LANES:(l + 1) * LANES]
        sre_ref[:, l * sw:(l + 1) * sw] = jnp.dot(ul, bre_ref[l], preferred_element_type=F32)
        sim_ref[:, l * sw:(l + 1) * sw] = jnp.dot(ul, bim_ref[l], preferred_element_type=F32)

    chunk = 1024
    for c0 in range(0, ntile * sw, chunk):
        cs = slice(c0, c0 + chunk)
        ar = are_ref[:, cs]
        ai = aim_ref[:, cs]

        def body(r, carry):
            sr, si = carry
            row = pl.ds(r, 1)
            nr = ar * sr - ai * si + sre_ref[row, cs]
            ni = ar * si + ai * sr + sim_ref[row, cs]
            sre_ref[row, cs] = nr
            sim_ref[row, cs] = ni
            return nr, ni

        sr, si = lax.fori_loop(0, tb, body, (cre_carry[0:1, cs], cim_carry[0:1, cs]))
        cre_carry[0:1, cs] = sr
        cim_carry[0:1, cs] = si

    ys = []
    for l in range(ntile):
        s_re = sre_ref[:, l * sw:(l + 1) * sw].astype(BF16)
        s_im = sim_ref[:, l * sw:(l + 1) * sw].astype(BF16)
        ys.append(jnp.dot(s_re, cre_ref[l], preferred_element_type=F32)
                  - jnp.dot(s_im, cim_ref[l], preferred_element_type=F32))
    y = jnp.concatenate(ys, axis=1) + d_ref[...] * u.astype(F32)
    g = jax.nn.gelu(y)
    gate = jax.nn.sigmoid(jnp.dot(g.astype(BF16), gw_ref[...], preferred_element_type=F32) + gb_ref[...])
    out = g * gate
    ms = jnp.mean(out * out, axis=-1, keepdims=True)
    o_ref[...] = (out * lax.rsqrt(ms + RMS_EPS) * gn_ref[...]).astype(BF16)


def _s5_params(lam_re, lam_im, log_dt, b_re, b_im, c_re, c_im):
    g, p = lam_re.shape
    gpt = LANES // SSM_GROUP
    nt = g // gpt
    dt = jnp.exp(log_dt)[:, None]
    mag = jnp.exp(lam_re * dt)
    ab_re, ab_im = mag * jnp.cos(lam_im * dt), mag * jnp.sin(lam_im * dt)
    den = lam_re * lam_re + lam_im * lam_im
    nr, ni = ab_re - 1.0, ab_im
    zr = (nr * lam_re + ni * lam_im) / den
    zi = (ni * lam_re - nr * lam_im) / den
    bb_re = zr[..., None] * b_re - zi[..., None] * b_im
    bb_im = zr[..., None] * b_im + zi[..., None] * b_re
    eye = jnp.eye(gpt, dtype=F32)

    def in_map(bb):
        bb = bb.reshape(nt, gpt, p, SSM_GROUP)
        m = jnp.einsum("lgpc,gh->lgchp", bb, eye)
        return m.reshape(nt, LANES, gpt * p).astype(BF16)

    def out_map(cc):
        cc = cc.reshape(nt, gpt, SSM_GROUP, p)
        m = jnp.einsum("lgcp,gh->lhpgc", cc, eye)
        return m.reshape(nt, gpt * p, LANES).astype(BF16)

    return (in_map(bb_re), in_map(bb_im), ab_re.reshape(1, g * p), ab_im.reshape(1, g * p),
            out_map(c_re), out_map(c_im))


def _s5_mixer(proj_a, params, d_skip, glu_w, glu_b, gain, bsz, tp):
    tb = SEQ_BLOCK
    b_re, b_im, a_re, a_im, c_re, c_im = params
    nt, _, sw = b_re.shape
    width = nt * LANES
    pa = proj_a.reshape(bsz, tp, proj_a.shape[-1])
    const = lambda shp: _resident(shp, lambda b, t: (0,) * len(shp))
    return pl.pallas_call(
        _s5_kernel, grid=(bsz, tp // tb),
        in_specs=[pl.BlockSpec((None, tb, width), lambda b, t: (b, t, 0)),
                  const(b_re.shape), const(b_im.shape), const(a_re.shape), const(a_im.shape),
                  const(c_re.shape), const(c_im.shape), const((1, width)), const((width, width)),
                  const((1, width)), const((1, width))],
        out_specs=pl.BlockSpec((None, tb, width), lambda b, t: (b, t, 0)),
        out_shape=jax.ShapeDtypeStruct((bsz, tp, width), BF16),
        scratch_shapes=[pltpu.VMEM((tb, nt * sw), F32), pltpu.VMEM((tb, nt * sw), F32),
                        pltpu.VMEM((SUBLANES, nt * sw), F32), pltpu.VMEM((SUBLANES, nt * sw), F32)],
        compiler_params=_cparams(("arbitrary", "arbitrary")), name="s5_mixer",
    )(pa, b_re, b_im, a_re, a_im, c_re, c_im, d_skip.reshape(1, width), glu_w.astype(BF16),
      glu_b.reshape(1, width), gain.reshape(1, width))


def _out_proj_kernel(alpha, ys_ref, yf_ref, yd_ref, w_ref, h_ref, g_ref, b_ref, o32_ref, o16_ref):
    n1 = ys_ref.shape[1]
    n2 = n1 + yf_ref.shape[1]
    m = (jnp.dot(ys_ref[...], w_ref[0:n1, :], preferred_element_type=F32)
         + jnp.dot(yf_ref[...], w_ref[n1:n2, :], preferred_element_type=F32)
         + jnp.dot(yd_ref[...], w_ref[n2:, :], preferred_element_type=F32))
    y = _ln_rows(alpha * h_ref[...] + m, g_ref[...], b_ref[...])
    o32_ref[...] = y
    o16_ref[...] = y.astype(BF16)


def _out_proj(y_ssm, y_fox, y_dsa, w_out, h, g, b, alpha):
    m, d = h.shape
    tm = _pick(m, (256, 128))
    row = lambda w: pl.BlockSpec((tm, w), lambda i: (i, 0))
    vec = pl.BlockSpec((1, d), lambda i: (0, 0))
    return pl.pallas_call(
        functools.partial(_out_proj_kernel, alpha), grid=(m // tm,),
        in_specs=[row(y_ssm.shape[1]), row(y_fox.shape[1]), row(y_dsa.shape[1]),
                  _resident((d, d), lambda i: (0, 0)), row(d), vec, vec],
        out_specs=(row(d), row(d)),
        out_shape=(jax.ShapeDtypeStruct((m, d), F32), jax.ShapeDtypeStruct((m, d), BF16)),
        compiler_params=_cparams(("parallel",)), name="out_proj",
    )(y_ssm, y_fox, y_dsa, w_out.astype(BF16), h, g.reshape(1, d), b.reshape(1, d))


HALO = 16


def _ffn_kernel(h_ref, halo_ref, wv_ref, wg_ref, cwv_ref, cwg_ref, cbv_ref, cbg_ref, wd_ref, o_ref):
    ti = pl.program_id(1)
    j = pl.program_id(2)
    h = h_ref[...]
    halo = halo_ref[...]
    tm = h.shape[0]
    first = ti == 0

    def conv(w_ref, cw_ref, cb_ref):
        a = jnp.dot(h, w_ref[...], preferred_element_type=F32)
        row = lax.broadcasted_iota(I32, a.shape, 0)
        ah = jnp.dot(halo, w_ref[...], preferred_element_type=F32)
        ah = jnp.where(first, 0.0, ah)
        p1 = jnp.where(row == 0, ah[HALO - 1:HALO, :], pltpu.roll(a, 1, axis=0))
        p2 = jnp.where(row == 0, ah[HALO - 2:HALO - 1, :],
                       jnp.where(row == 1, ah[HALO - 1:HALO, :], pltpu.roll(a, 2, axis=0)))
        cw = cw_ref[...]
        return cw[0:1, :] * p2 + cw[1:2, :] * p1 + cw[2:3, :] * a + cb_ref[...]

    val = conv(wv_ref, cwv_ref, cbv_ref)
    gate = conv(wg_ref, cwg_ref, cbg_ref)
    act = (jax.nn.silu(gate) * val).astype(BF16)
    contrib = jnp.dot(act, wd_ref[...], preferred_element_type=F32)

    @pl.when(j == 0)
    def _():
        o_ref[...] = contrib

    @pl.when(j > 0)
    def _():
        o_ref[...] += contrib


def _conv_ffn(h16, w_up, conv_w, conv_b, w_down, bsz, tp):
    d = h16.shape[-1]
    dff = w_down.shape[0]
    tm = _pick(tp, (768, 512, 256))
    tf = _pick(dff, (512, 256, 128))
    nf = dff // tf
    hb = h16.reshape(bsz, tp, d)
    w_up16 = w_up.astype(BF16)
    cw8 = jnp.zeros((SUBLANES, 2 * dff), F32).at[:CONV_WIDTH].set(conv_w)
    cb = conv_b.reshape(1, 2 * dff)
    halo_blocks = tm // HALO
    return pl.pallas_call(
        _ffn_kernel, grid=(bsz, tp // tm, nf),
        in_specs=[pl.BlockSpec((None, tm, d), lambda b, i, j: (b, i, 0)),
                  pl.BlockSpec((None, HALO, d), lambda b, i, j: (b, jnp.maximum(i * halo_blocks - 1, 0), 0)),
                  pl.BlockSpec((d, tf), lambda b, i, j: (0, j)),
                  pl.BlockSpec((d, tf), lambda b, i, j: (0, j + nf)),
                  pl.BlockSpec((SUBLANES, tf), lambda b, i, j: (0, j)),
                  pl.BlockSpec((SUBLANES, tf), lambda b, i, j: (0, j + nf)),
                  pl.BlockSpec((1, tf), lambda b, i, j: (0, j)),
                  pl.BlockSpec((1, tf), lambda b, i, j: (0, j + nf)),
                  pl.BlockSpec((tf, d), lambda b, i, j: (j, 0))],
        out_specs=pl.BlockSpec((None, tm, d), lambda b, i, j: (b, i, 0)),
        out_shape=jax.ShapeDtypeStruct((bsz, tp, d), F32),
        compiler_params=_cparams(("parallel", "parallel", "arbitrary")), name="conv_ffn",
    )(hb, hb, w_up16, w_up16, cw8, cw8, cb, cb, w_down.astype(BF16)).reshape(bsz * tp, d)


def _split_w_in(w_in, d_model):
    ssm_w = d_model // 2
    fox_w = d_model // 4
    dsa_w = d_model - ssm_w - fox_w
    kv_rank = d_model // 16
    fox_h = fox_w // HEAD_DIM
    sizes = (ssm_w, fox_w, fox_w, fox_w, fox_h, dsa_w, kv_rank, IDX_HEADS * IDX_DIM, IDX_DIM, IDX_HEADS)
    offs = np.cumsum((0,) + sizes)
    seg = lambda k: w_in[:, offs[k]:offs[k + 1]]
    u, fq, fk, fv, ff, dq, ckv, iq, ik, iw = (seg(k) for k in range(10))
    w_a = jnp.concatenate([u, fq, fk, fv, dq, iq, ckv], axis=1).astype(BF16)
    pad = jnp.zeros((w_in.shape[0], LANES - IDX_DIM - fox_h - IDX_HEADS), w_in.dtype)
    w_b = jnp.concatenate([ik, ff, iw, pad], axis=1).astype(BF16)
    return w_a, w_b


def _heads(x, bsz, tp):
    return x.reshape(bsz, tp, -1, HEAD_DIM).transpose(0, 2, 1, 3)


def _values_t(v, bsz, tp):
    vt = v.reshape(bsz, tp, -1, HEAD_DIM).transpose(0, 2, 3, 1)
    nh = vt.shape[1]
    ones = jnp.ones((bsz, nh, 1, tp), v.dtype)
    zeros = jnp.zeros((bsz, nh, AUG_ROWS - HEAD_DIM - 1, tp), v.dtype)
    return jnp.concatenate([vt, ones, zeros], axis=2)


def _mixer(h16, h32, lp, cos_t, sin_t, topk, bsz, tp, alpha):
    d = h32.shape[-1]
    w_a, w_b = _split_w_in(lp["w_in"], d)
    proj_a = _matmul(h16, w_a, BF16)
    proj_b = _matmul(h16, w_b, F32)
    dq, dk, dv, iq, misc, cum = _prep(proj_a, proj_b, cos_t, sin_t, lp["kv_norm_g"], lp["w_kv_up"],
                                      lp["fox_f_bias"], bsz, tp)
    ssm_w = d // 2
    fox_w = d // 4
    g_ssm, g_fox, g_dsa = jnp.split(lp["mix_norm_g"], (ssm_w, ssm_w + fox_w))
    tq = SEQ_BLOCK

    y_ssm = _s5_mixer(proj_a, _s5_params(lp["ssm_lam_re"], lp["ssm_lam_im"], lp["ssm_log_dt"], lp["ssm_b_re"],
                                         lp["ssm_b_im"], lp["ssm_c_re"], lp["ssm_c_im"]),
                      lp["ssm_d"], lp["ssm_glu_w"], lp["ssm_glu_b"], g_ssm, bsz, tp)

    pa = proj_a.reshape(bsz, tp, -1)
    fq = _heads(pa[..., ssm_w:ssm_w + fox_w], bsz, tp) * jnp.asarray(HEAD_DIM ** -0.5, BF16)
    fk = _heads(pa[..., ssm_w + fox_w:ssm_w + 2 * fox_w], bsz, tp)
    fv = pa[..., ssm_w + 2 * fox_w:ssm_w + 3 * fox_w]
    nfh = fox_w // HEAD_DIM
    cum_h = cum[..., IDX_DIM:IDX_DIM + nfh].transpose(0, 2, 1)[..., None]
    hi, mid, lo = _split3(cum_h)
    one = jnp.ones_like(hi)
    zpad = jnp.zeros((bsz, nfh, tp, LANES - HEAD_DIM - 6), BF16)
    q_aug = jnp.concatenate([fq, hi, mid, lo, one, one, one, zpad], axis=-1)
    k_aug = jnp.concatenate([fk, one, one, one, -hi, -mid, -lo, zpad], axis=-1)
    gain = lambda g: jnp.broadcast_to(g[:, None], (g.shape[0], tq))
    y_fox = _fox_attention(q_aug, k_aug, _values_t(fv, bsz, tp), gain(g_fox), bsz, tp)

    nq = tp // tq
    iq_hm = iq.reshape(bsz, nq, tq, IDX_HEADS, IDX_DIM).transpose(0, 1, 3, 2, 4).reshape(
        bsz, nq, IDX_HEADS * tq, IDX_DIM)
    ik = misc[..., :IDX_DIM].astype(BF16)
    iw_t = misc[..., IDX_DIM + nfh:IDX_DIM + nfh + IDX_HEADS].transpose(0, 2, 1)
    ndh = dq.shape[-1] // HEAD_DIM
    lane_head = (np.arange(LANES) // HEAD_DIM)[None, :]
    own = jnp.asarray(lane_head == (np.arange(ndh) % 2)[:, None])
    q_pair = dq.reshape(bsz, tp, ndh // 2, LANES).transpose(0, 2, 1, 3)
    q_pad = jnp.where(own[None, :, None, :], jnp.repeat(q_pair, 2, axis=1), jnp.zeros((), BF16))
    k_pair = dk.reshape(bsz, tp, ndh // 2, LANES).transpose(0, 2, 1, 3)
    y_dsa = _dsa_attention(iq_hm, iw_t, q_pad, ik, k_pair, _values_t(dv, bsz, tp), gain(g_dsa), topk, bsz, tp)

    return _out_proj(y_ssm.reshape(bsz * tp, -1), y_fox.reshape(bsz * tp, -1), y_dsa.reshape(bsz * tp, -1),
                     lp["w_out"], h32, lp["ln1_g"], lp["ln1_b"], alpha)


def kernel(x, meta_tokens, ln_in_g, ln_in_b, w_in, ssm_lam_re, ssm_lam_im, ssm_log_dt, ssm_b_re, ssm_b_im,
           ssm_c_re, ssm_c_im, ssm_d, ssm_glu_w, ssm_glu_b, fox_f_bias, kv_norm_g, w_kv_up, mix_norm_g, w_out,
           ln1_g, ln1_b, w_up, conv_w, conv_b, w_down, ln2_g, ln2_b):
    bsz, seq, d = x.shape
    depth = w_in.shape[0]
    alpha = (2.0 * depth) ** 0.25
    topk = min(TOPK_MAX, seq // 4)
    t_real = seq + N_META
    tp = -(-t_real // SEQ_BLOCK) * SEQ_BLOCK

    meta = jnp.broadcast_to(meta_tokens[None].astype(x.dtype), (bsz, N_META, d))
    xcat = jnp.concatenate([meta, x, jnp.zeros((bsz, tp - t_real, d), x.dtype)], axis=1)
    h32, h16 = _layer_norm(xcat.reshape(bsz * tp, d), ln_in_g, ln_in_b)
    cos_t, sin_t = _rope_tables(tp)

    stacked = dict(w_in=w_in, ssm_lam_re=ssm_lam_re, ssm_lam_im=ssm_lam_im, ssm_log_dt=ssm_log_dt,
                   ssm_b_re=ssm_b_re, ssm_b_im=ssm_b_im, ssm_c_re=ssm_c_re, ssm_c_im=ssm_c_im, ssm_d=ssm_d,
                   ssm_glu_w=ssm_glu_w, ssm_glu_b=ssm_glu_b, fox_f_bias=fox_f_bias, kv_norm_g=kv_norm_g,
                   w_kv_up=w_kv_up, mix_norm_g=mix_norm_g, w_out=w_out, ln1_g=ln1_g, ln1_b=ln1_b, w_up=w_up,
                   conv_w=conv_w, conv_b=conv_b, w_down=w_down, ln2_g=ln2_g, ln2_b=ln2_b)
    for l in range(depth):
        lp = {k: v[l] for k, v in stacked.items()}
        h32, h16 = _mixer(h16, h32, lp, cos_t, sin_t, topk, bsz, tp, alpha)
        f = _conv_ffn(h16, lp["w_up"], lp["conv_w"], lp["conv_b"], lp["w_down"], bsz, tp)
        h32, h16 = _layer_norm(h32, lp["ln2_g"], lp["ln2_b"], f=f, alpha=alpha)
    return h32.reshape(bsz, tp, d)[:, N_META:t_real, :]
```

```python
import functools
import math

import jax
import jax.numpy as jnp
import numpy as np
from jax import lax
from jax.experimental import pallas as pl
from jax.experimental.pallas import tpu as pltpu

F32 = jnp.float32
BF16 = jnp.bfloat16
I32 = jnp.int32
I16 = jnp.int16
HALF_BIAS = 1 << 15

N_META = 16
HEAD_DIM = 64
SSM_GROUP = 16
SSM_STATE = 64
IDX_HEADS = 16
IDX_DIM = 64
TOPK_MAX = 256
ROPE_THETA = 500000.0
ROT_DIM = HEAD_DIM // 4
CONV_WIDTH = 3
LN_EPS = 1e-5
RMS_EPS = 1e-6
NEG_INF = -1e30

LANES = 128
SUBLANES = 8
VMEM_LIMIT_BYTES = 56 * 1024 * 1024

SEQ_BLOCK = 256
AUG_ROWS = 80
INT_MIN = -(2 ** 31)


def _pick(n, candidates):
    for c in candidates:
        if n % c == 0:
            return c
    raise ValueError(f"no tile for {n} in {candidates}")


def _cparams(sem, vmem=VMEM_LIMIT_BYTES):
    return pltpu.CompilerParams(dimension_semantics=sem, vmem_limit_bytes=vmem)


def _resident(shape, index_map):
    return pl.BlockSpec(shape, index_map, pipeline_mode=pl.Buffered(1))


def _ln_rows(z, g, b):
    mu = jnp.mean(z, axis=-1, keepdims=True)
    zc = z - mu
    var = jnp.mean(zc * zc, axis=-1, keepdims=True)
    return zc * lax.rsqrt(var + LN_EPS) * g + b


def _ln_kernel(x_ref, g_ref, b_ref, o32_ref, o16_ref):
    y = _ln_rows(x_ref[...], g_ref[...], b_ref[...])
    o32_ref[...] = y
    o16_ref[...] = y.astype(BF16)


def _add_ln_kernel(alpha, h_ref, f_ref, g_ref, b_ref, o32_ref, o16_ref):
    y = _ln_rows(alpha * h_ref[...] + f_ref[...], g_ref[...], b_ref[...])
    o32_ref[...] = y
    o16_ref[...] = y.astype(BF16)


def _layer_norm(x, g, b, f=None, alpha=1.0):
    m, d = x.shape
    tm = _pick(m, (512, 256, 128))
    row = pl.BlockSpec((tm, d), lambda i: (i, 0))
    vec = pl.BlockSpec((1, d), lambda i: (0, 0))
    outs = (jax.ShapeDtypeStruct((m, d), F32), jax.ShapeDtypeStruct((m, d), BF16))
    if f is None:
        return pl.pallas_call(
            _ln_kernel, grid=(m // tm,), in_specs=[row, vec, vec], out_specs=(row, row),
            out_shape=outs, compiler_params=_cparams(("parallel",)), name="layer_norm",
        )(x, g.reshape(1, d), b.reshape(1, d))
    return pl.pallas_call(
        functools.partial(_add_ln_kernel, alpha), grid=(m // tm,), in_specs=[row, row, vec, vec],
        out_specs=(row, row), out_shape=outs, compiler_params=_cparams(("parallel",)),
        name="residual_layer_norm",
    )(x, f, g.reshape(1, d), b.reshape(1, d))


def _mm_kernel(a_ref, w_ref, o_ref):
    o_ref[...] = jnp.dot(a_ref[...], w_ref[...], preferred_element_type=F32).astype(o_ref.dtype)


def _matmul(a, w, out_dtype):
    m, k = a.shape
    n = w.shape[1]
    tm = _pick(m, (1536, 1024, 768, 512, 256))
    tn = _pick(n, (1408, 1024, 512, 256, 128))
    return pl.pallas_call(
        _mm_kernel, grid=(m // tm, n // tn),
        in_specs=[pl.BlockSpec((tm, k), lambda i, j: (i, 0)), pl.BlockSpec((k, tn), lambda i, j: (0, j))],
        out_specs=pl.BlockSpec((tm, tn), lambda i, j: (i, j)),
        out_shape=jax.ShapeDtypeStruct((m, n), out_dtype),
        compiler_params=_cparams(("parallel", "parallel")), name="matmul",
    )(a, w)


def _rope_perm():
    p = np.zeros((LANES, LANES), np.float32)
    half = ROT_DIM // 2
    for c in range(LANES):
        cc = c % HEAD_DIM
        if cc < half:
            p[c + half, c] = -1.0
        elif cc < ROT_DIM:
            p[c - half, c] = 1.0
    return jnp.asarray(p, BF16)


def _rope_tables(tp):
    pos = jnp.arange(tp, dtype=F32)
    inv_freq = ROPE_THETA ** (-jnp.arange(0, ROT_DIM, 2, dtype=F32) / ROT_DIM)
    ang = pos[:, None] * inv_freq[None, :]
    cos, sin = jnp.cos(ang), jnp.sin(ang)
    half = ROT_DIM // 2
    lane = np.arange(LANES) % HEAD_DIM
    rot = jnp.asarray(lane < ROT_DIM)
    idx = jnp.asarray(lane % half)
    cos_t = jnp.where(rot[None, :], cos[:, idx], 1.0)
    sin_t = jnp.where(rot[None, :], sin[:, idx], 0.0)
    return cos_t, sin_t


def _split3(x):
    hi = x.astype(BF16)
    r1 = x - hi.astype(F32)
    mid = r1.astype(BF16)
    lo = (r1 - mid.astype(F32)).astype(BF16)
    return hi, mid, lo


def _placement_constants(n_fox_heads):
    nh = n_fox_heads
    sq = np.zeros((nh * HEAD_DIM, nh * LANES), np.float32)
    for h in range(nh):
        for d in range(HEAD_DIM):
            sq[h * HEAD_DIM + d, h * LANES + d] = 1.0
    eq = np.zeros((3 * LANES, nh * LANES), np.float32)
    ek = np.zeros((3 * LANES, nh * LANES), np.float32)
    cq = np.zeros((1, nh * LANES), np.float32)
    ck = np.zeros((1, nh * LANES), np.float32)
    for h in range(nh):
        for j in range(3):
            eq[j * LANES + IDX_DIM + h, h * LANES + HEAD_DIM + j] = 1.0
            ek[j * LANES + IDX_DIM + h, h * LANES + HEAD_DIM + 3 + j] = -1.0
            cq[0, h * LANES + HEAD_DIM + 3 + j] = 1.0
            ck[0, h * LANES + HEAD_DIM + j] = 1.0
    dup = np.zeros((LANES, LANES), np.float32)
    for d in range(IDX_DIM):
        dup[d, d] = 1.0
        dup[d, d + IDX_DIM] = 1.0
    bf = lambda a: jnp.asarray(a, BF16)
    return bf(sq * HEAD_DIM ** -0.5), bf(sq), bf(eq), bf(ek), jnp.asarray(cq), jnp.asarray(ck), bf(dup)


def _prep_kernel(fq_ref, fk_ref, fv_ref, dq_ref, iq_ref, ckv_ref, misc_ref, cos_ref, sin_ref, perm_ref,
                 kvg_ref, wkv_ref, fb_ref, sq_ref, sk_ref, eq_ref, ek_ref, cq_ref, ck_ref, dup_ref,
                 qa_o, ka_o, fvt_o, dqp_o, dkp_o, dvt_o, iqp_o, ik2_o, iwt_o, carry_ref):
    t = pl.program_id(1)
    cos = cos_ref[...]
    sin = sin_ref[...]
    perm = perm_ref[...]
    tb = cos.shape[0]
    lane = lax.broadcasted_iota(I32, (tb, LANES), 1)
    lower = lane < HEAD_DIM

    def rope_tiles(x):
        outs = []
        for j in range(x.shape[1] // LANES):
            xt = x[:, j * LANES:(j + 1) * LANES]
            yt = jnp.dot(xt.astype(BF16), perm, preferred_element_type=F32)
            outs.append(xt * cos + yt * sin)
        return outs

    def pair_split(tile, scale, out_ref, p):
        v = (tile * scale).astype(BF16)
        zero = jnp.zeros_like(v)
        out_ref[2 * p] = jnp.where(lower, v, zero)
        out_ref[2 * p + 1] = jnp.where(lower, zero, v)

    def values_t(v, out_ref):
        sub = lax.broadcasted_iota(I32, (AUG_ROWS - HEAD_DIM, tb), 0)
        tail = jnp.where(sub == 0, 1.0, 0.0).astype(BF16)
        for p in range(v.shape[1] // LANES):
            vt = v[:, p * LANES:(p + 1) * LANES].astype(F32).T
            for q in range(2):
                out_ref[2 * p + q, 0:HEAD_DIM, :] = vt[q * HEAD_DIM:(q + 1) * HEAD_DIM, :].astype(BF16)
                out_ref[2 * p + q, HEAD_DIM:AUG_ROWS, :] = tail

    for p, tile in enumerate(rope_tiles(dq_ref[...].astype(F32))):
        pair_split(tile, HEAD_DIM ** -0.5, dqp_o, p)
    for p, tile in enumerate(rope_tiles(iq_ref[...].astype(F32))):
        pair_split(tile, IDX_DIM ** -0.5, iqp_o, p)
    ckv = ckv_ref[...].astype(F32)
    ckvn = ckv * lax.rsqrt(jnp.mean(ckv * ckv, axis=-1, keepdims=True) + RMS_EPS) * kvg_ref[...]
    kv = jnp.dot(ckvn.astype(BF16), wkv_ref[...], preferred_element_type=F32)
    half = kv.shape[1] // 2
    for p, tile in enumerate(rope_tiles(kv[:, :half])):
        dkp_o[p] = tile.astype(BF16)
    values_t(kv[:, half:].astype(BF16), dvt_o)
    values_t(fv_ref[...], fvt_o)

    x = misc_ref[...]
    ik = rope_tiles(x)[0].astype(BF16)
    ik2_o[...] = jnp.dot(ik, dup_ref[...], preferred_element_type=F32).astype(BF16)
    nfh = fq_ref.shape[1] // HEAD_DIM
    iwt_o[...] = (x * (IDX_HEADS ** -0.5)).T[IDX_DIM + nfh:IDX_DIM + nfh + IDX_HEADS, :]
    logf = jax.nn.log_sigmoid(x + fb_ref[...])
    logf = jnp.where((lane >= IDX_DIM) & (lane < IDX_DIM + nfh), logf, 0.0)

    @pl.when(t == 0)
    def _():
        carry_ref[...] = jnp.zeros_like(carry_ref)

    r = lax.broadcasted_iota(I32, (tb, tb), 0)
    c = lax.broadcasted_iota(I32, (tb, tb), 1)
    tril = jnp.where(r >= c, 1.0, 0.0).astype(BF16)
    hi, mid, lo = _split3(logf)
    cum = (jnp.dot(tril, hi, preferred_element_type=F32) + jnp.dot(tril, mid, preferred_element_type=F32)
           + jnp.dot(tril, lo, preferred_element_type=F32)) + carry_ref[0:1, :]
    carry_ref[0:1, :] = cum[tb - 1:tb, :]

    cum3 = jnp.concatenate(_split3(cum), axis=1)
    qa = (jnp.dot(fq_ref[...], sq_ref[...], preferred_element_type=F32)
          + jnp.dot(cum3, eq_ref[...], preferred_element_type=F32) + cq_ref[...]).astype(BF16)
    ka = (jnp.dot(fk_ref[...], sk_ref[...], preferred_element_type=F32)
          + jnp.dot(cum3, ek_ref[...], preferred_element_type=F32) + ck_ref[...]).astype(BF16)
    for h in range(nfh):
        qa_o[h] = qa[:, h * LANES:(h + 1) * LANES]
        ka_o[h] = ka[:, h * LANES:(h + 1) * LANES]


def _prep(proj_a, proj_b, cos_t, sin_t, kv_norm_g, w_kv_up, f_bias, bsz, tp):
    tb = SEQ_BLOCK
    na = proj_a.shape[-1]
    pa = proj_a.reshape(bsz, tp, na)
    pb = proj_b.reshape(bsz, tp, LANES)
    fb = jnp.zeros((1, LANES), F32).at[0, IDX_DIM:IDX_DIM + 8].set(f_bias)
    nfh = f_bias.shape[0]
    ndh = 512 // HEAD_DIM
    consts = _placement_constants(nfh)
    bs = lambda w, cb: pl.BlockSpec((None, tb, w), lambda b, t: (b, t, cb))
    tab = pl.BlockSpec((tb, LANES), lambda b, t: (t, 0))
    const = lambda a: pl.BlockSpec(a.shape, lambda b, t: (0,) * a.ndim)
    heads = lambda n: pl.BlockSpec((None, n, tb, LANES), lambda b, t: (b, 0, t, 0))
    heads_t = lambda n: pl.BlockSpec((None, n, AUG_ROWS, tb), lambda b, t: (b, 0, 0, t))
    sds = jax.ShapeDtypeStruct
    out_shapes = (
        sds((bsz, nfh, tp, LANES), BF16), sds((bsz, nfh, tp, LANES), BF16), sds((bsz, nfh, AUG_ROWS, tp), BF16),
        sds((bsz, ndh, tp, LANES), BF16), sds((bsz, ndh // 2, tp, LANES), BF16), sds((bsz, ndh, AUG_ROWS, tp), BF16),
        sds((bsz, IDX_HEADS, tp, LANES), BF16), sds((bsz, tp, LANES), BF16), sds((bsz, IDX_HEADS, tp), F32),
    )
    out_specs = (heads(nfh), heads(nfh), heads_t(nfh), heads(ndh), heads(ndh // 2), heads_t(ndh),
                 heads(IDX_HEADS), bs(LANES, 0), pl.BlockSpec((None, IDX_HEADS, tb), lambda b, t: (b, 0, t)))
    fixed = (_rope_perm(), kv_norm_g.reshape(1, LANES), w_kv_up.astype(BF16), fb) + consts
    return pl.pallas_call(
        _prep_kernel, grid=(bsz, tp // tb),
        in_specs=[bs(512, 2), bs(512, 3), bs(512, 4), bs(512, 5), bs(1024, 3), bs(LANES, 32), bs(LANES, 0),
                  tab, tab] + [const(a) for a in fixed],
        out_specs=out_specs, out_shape=out_shapes, scratch_shapes=[pltpu.VMEM((SUBLANES, LANES), F32)],
        compiler_params=_cparams(("arbitrary", "arbitrary")), name="prep",
    )(pa, pa, pa, pa, pa, pa, pb, cos_t, sin_t, *fixed)


_NT = (((1,), (1,)), ((), ()))


def _attend(q_ref, k_at, v_at, bias_at, n_full, diag_bias, m_ref, acc_ref):
    nh = q_ref.shape[0]
    m_ref[...] = jnp.full(m_ref.shape, NEG_INF, F32)
    acc_ref[...] = jnp.zeros(acc_ref.shape, F32)

    tq = q_ref.shape[1]

    def step(kb, tk, masked):
        rows = pl.ds(pl.multiple_of(kb * tk, tk), tk)
        bias = bias_at(rows)
        if masked and diag_bias is not None:
            bias = diag_bias if bias is None else bias + diag_bias
        logits = lambda h: lax.dot_general(k_at(h, rows), q_ref[h], _NT, preferred_element_type=F32)
        ahead = 2
        pending = {h: logits(h) for h in range(min(ahead, nh))}
        for h in range(nh):
            s = pending.pop(h)
            if h + ahead < nh:
                pending[h + ahead] = logits(h + ahead)
            if bias is not None:
                s = bias + s
            m_old = m_ref[h, 0:1, :]
            m_new = jnp.maximum(m_old, jnp.max(s, axis=0, keepdims=True))
            p = jnp.exp(s - m_new)
            acc_ref[h] = (acc_ref[h] * jnp.exp(m_old - m_new)
                          + jnp.dot(v_at(h, rows), p.astype(BF16), preferred_element_type=F32))
            m_ref[h, 0:1, :] = m_new

    def loop_body(tk, masked, kb, carry):
        step(kb, tk, masked)
        return carry

    n_wide = lax.shift_right_logical(n_full, 1)
    lax.fori_loop(0, n_wide, functools.partial(loop_body, 2 * tq, False), 0)
    lax.fori_loop(2 * n_wide, n_full, functools.partial(loop_body, tq, False), 0)
    lax.fori_loop(n_full, n_full + 1, functools.partial(loop_body, tq, True), 0)
    return jnp.concatenate([acc_ref[h, :HEAD_DIM, :] / acc_ref[h, HEAD_DIM:HEAD_DIM + 1, :] for h in range(nh)],
                           axis=0)


def _diag_bias(tk, tq):
    r = lax.broadcasted_iota(I32, (tk, tq), 0)
    c = lax.broadcasted_iota(I32, (tk, tq), 1)
    return jnp.where(r <= c, 0.0, NEG_INF)


def _group_norm_t(y_t, gain_t):
    ms = jnp.mean(y_t * y_t, axis=0, keepdims=True)
    return (y_t * lax.rsqrt(ms + RMS_EPS) * gain_t).T.astype(BF16)


def _fox_kernel(q_ref, k_ref, v_ref, g_ref, o_ref, m_ref, acc_ref):
    i = pl.program_id(1)
    tq = q_ref.shape[1]
    y_t = _attend(q_ref, lambda h, rows: k_ref[h, rows, :], lambda h, rows: v_ref[h, :, rows],
                  lambda rows: None, i, _diag_bias(tq, tq), m_ref, acc_ref)
    o_ref[...] = _group_norm_t(y_t, g_ref[...])


def _fox_attention(q_aug, k_aug, v_t, gain_t, bsz, tp):
    tq = SEQ_BLOCK
    nh = q_aug.shape[1]
    return pl.pallas_call(
        _fox_kernel, grid=(bsz, tp // tq),
        in_specs=[pl.BlockSpec((None, nh, tq, LANES), lambda b, i: (b, 0, i, 0)),
                  _resident((None, nh, tp, LANES), lambda b, i: (b, 0, 0, 0)),
                  _resident((None, nh, AUG_ROWS, tp), lambda b, i: (b, 0, 0, 0)),
                  _resident((nh * HEAD_DIM, tq), lambda b, i: (0, 0))],
        out_specs=pl.BlockSpec((None, tq, nh * HEAD_DIM), lambda b, i: (b, i, 0)),
        out_shape=jax.ShapeDtypeStruct((bsz, tp, nh * HEAD_DIM), BF16),
        scratch_shapes=[pltpu.VMEM((nh, SUBLANES, tq), F32), pltpu.VMEM((nh, AUG_ROWS, tq), F32)],
        compiler_params=_cparams(("arbitrary", "arbitrary")), name="fox_attention",
    )(q_aug, k_aug, v_t, gain_t)


def _order_key(x):
    b = lax.bitcast_convert_type(x + 0.0, I32)
    return b ^ ((b >> 31) & 0x7FFFFFFF)


def _dsa_kernel(topk, iq_ref, iw_ref, q_ref, ik_ref, k_ref, v_ref, g_ref, o_ref, key_ref, hi_ref, lo_ref,
                m_ref, acc_ref):
    i = pl.program_id(1)
    tq = q_ref.shape[1]
    tk = tq
    diag_bias = _diag_bias(tk, tq)
    blk = lambda kb: pl.ds(pl.multiple_of(kb * tk, tk), tk)
    heads_per_dot = 4

    def score_block(kb, masked):
        ik = ik_ref[blk(kb), :]
        sc = jnp.zeros((tk, tq), F32)
        for j0 in range(0, IDX_HEADS, heads_per_dot):
            iq = iq_ref[j0:j0 + heads_per_dot].reshape(heads_per_dot * tq, LANES)
            lg = lax.dot_general(ik, iq, _NT, preferred_element_type=F32)
            for j in range(heads_per_dot):
                sc = sc + jnp.maximum(lg[:, j * tq:(j + 1) * tq], 0.0) * iw_ref[j0 + j:j0 + j + 1, :]
        if masked:
            sc = jnp.where(diag_bias == 0.0, sc, NEG_INF)
        key = _order_key(sc)
        key_ref[blk(kb), :] = key
        hi_ref[blk(kb), :] = (key >> 16).astype(I16)
        lo_ref[blk(kb), :] = ((key & 0xFFFF) - HALF_BIAS).astype(I16)

    def score_loop(kb, carry):
        score_block(kb, False)
        return carry

    lax.fori_loop(0, i, score_loop, 0)
    score_block(i, True)

    pack = 2 * SUBLANES

    def count(ref, pred):
        def body(kb, cnt):
            hit = jnp.where(pred(ref[blk(kb), :]), jnp.ones((), I16), jnp.zeros((), I16))
            for g in range(tk // pack):
                cnt = cnt + hit[g * pack:(g + 1) * pack, :]
            return cnt
        cnt = lax.fori_loop(0, i + 1, body, jnp.zeros((pack, tq), I16))
        return jnp.sum(cnt.astype(I32), axis=0, keepdims=True)

    def kth_largest_half(ref, rank):
        t_u = jnp.zeros((1, tq), I32)
        for bit in range(15, -1, -1):
            cand_u = t_u | (1 << bit)
            cand = (cand_u - HALF_BIAS).astype(I16)
            t_u = jnp.where(count(ref, lambda x: x >= cand) >= rank, cand_u, t_u)
        return t_u - HALF_BIAS

    top_hi = kth_largest_half(hi_ref, topk)
    top_hi16 = top_hi.astype(I16)
    above = count(hi_ref, lambda x: x > top_hi16)

    def keep_lo(kb, carry):
        lo_ref[blk(kb), :] = jnp.where(hi_ref[blk(kb), :] == top_hi16, lo_ref[blk(kb), :],
                                       jnp.asarray(-HALF_BIAS, I16))
        return carry

    lax.fori_loop(0, i + 1, keep_lo, 0)
    top_lo = kth_largest_half(lo_ref, topk - above)
    thr = top_hi * 65536 + (top_lo + HALF_BIAS)
    floor_key = int(np.array(NEG_INF, np.float32).view(np.int32)) ^ 0x7FFFFFFF
    thr = jnp.maximum(thr, np.int32(floor_key + 1))

    y_t = _attend(q_ref, lambda h, rows: k_ref[h // 2, rows, :], lambda h, rows: v_ref[h, :, rows],
                  lambda rows: jnp.where(key_ref[rows, :] >= thr, 0.0, NEG_INF), i, None, m_ref, acc_ref)
    o_ref[...] = _group_norm_t(y_t, g_ref[...])


def _dsa_attention(iq_hm, iw_t, q_pad, ik, k_pair, v_t, gain_t, topk, bsz, tp):
    tq = SEQ_BLOCK
    nh = q_pad.shape[1]
    return pl.pallas_call(
        functools.partial(_dsa_kernel, topk), grid=(bsz, tp // tq),
        in_specs=[pl.BlockSpec((None, IDX_HEADS, tq, LANES), lambda b, i: (b, 0, i, 0)),
                  pl.BlockSpec((None, IDX_HEADS, tq), lambda b, i: (b, 0, i)),
                  pl.BlockSpec((None, nh, tq, LANES), lambda b, i: (b, 0, i, 0)),
                  _resident((None, tp, LANES), lambda b, i: (b, 0, 0)),
                  _resident((None, nh // 2, tp, LANES), lambda b, i: (b, 0, 0, 0)),
                  _resident((None, nh, AUG_ROWS, tp), lambda b, i: (b, 0, 0, 0)),
                  _resident((nh * HEAD_DIM, tq), lambda b, i: (0, 0))],
        out_specs=pl.BlockSpec((None, tq, nh * HEAD_DIM), lambda b, i: (b, i, 0)),
        out_shape=jax.ShapeDtypeStruct((bsz, tp, nh * HEAD_DIM), BF16),
        scratch_shapes=[pltpu.VMEM((tp, tq), I32), pltpu.VMEM((tp, tq), I16), pltpu.VMEM((tp, tq), I16),
                        pltpu.VMEM((nh, SUBLANES, tq), F32), pltpu.VMEM((nh, AUG_ROWS, tq), F32)],
        compiler_params=_cparams(("arbitrary", "arbitrary")), name="dsa_attention",
    )(iq_hm, iw_t, q_pad, ik, k_pair, v_t, gain_t)


def _s5_kernel(u_ref, bre_ref, bim_ref, are_ref, aim_ref, cre_ref, cim_ref, d_ref, gw_ref, gb_ref, gn_ref,
               o_ref, sre_ref, sim_ref, cre_carry, cim_carry):
    t = pl.program_id(1)
    tb = u_ref.shape[0]
    ntile = bre_ref.shape[0]
    sw = bre_ref.shape[2]

    @pl.when(t == 0)
    def _():
        cre_carry[...] = jnp.zeros_like(cre_carry)
        cim_carry[...] = jnp.zeros_like(cim_carry)

    u = u_ref[...]
    for l in range(ntile):
        ul = u[:, l * LANES:(l + 1) * LANES]
        sre_ref[:, l * sw:(l + 1) * sw] = jnp.dot(ul, bre_ref[l], preferred_element_type=F32)
        sim_ref[:, l * sw:(l + 1) * sw] = jnp.dot(ul, bim_ref[l], preferred_element_type=F32)

    chunk = 1024
    for c0 in range(0, ntile * sw, chunk):
        cs = slice(c0, c0 + chunk)
        ar = are_ref[:, cs]
        ai = aim_ref[:, cs]

        def body(r, carry):
            sr, si = carry
            row = pl.ds(r, 1)
            nr = ar * sr - ai * si + sre_ref[row, cs]
            ni = ar * si + ai * sr + sim_ref[row, cs]
            sre_ref[row, cs] = nr
            sim_ref[row, cs] = ni
            return nr, ni

        sr, si = lax.fori_loop(0, tb, body, (cre_carry[0:1, cs], cim_carry[0:1, cs]))
        cre_carry[0:1, cs] = sr
        cim_carry[0:1, cs] = si

    ys = []
    for l in range(ntile):
        s_re = sre_ref[:, l * sw:(l + 1) * sw].astype(BF16)
        s_im = sim_ref[:, l * sw:(l + 1) * sw].astype(BF16)
        ys.append(jnp.dot(s_re, cre_ref[l], preferred_element_type=F32)
                  - jnp.dot(s_im, cim_ref[l], preferred_element_type=F32))
    y = jnp.concatenate(ys, axis=1) + d_ref[...] * u.astype(F32)
    g = jax.nn.gelu(y)
    gate = jax.nn.sigmoid(jnp.dot(g.astype(BF16), gw_ref[...], preferred_element_type=F32) + gb_ref[...])
    out = g * gate
    ms = jnp.mean(out * out, axis=-1, keepdims=True)
    o_ref[...] = (out * lax.rsqrt(ms + RMS_EPS) * gn_ref[...]).astype(BF16)


def _s5_params(lam_re, lam_im, log_dt, b_re, b_im, c_re, c_im):
    g, p = lam_re.shape
    gpt = LANES // SSM_GROUP
    nt = g // gpt
    dt = jnp.exp(log_dt)[:, None]
    mag = jnp.exp(lam_re * dt)
    ab_re, ab_im = mag * jnp.cos(lam_im * dt), mag * jnp.sin(lam_im * dt)
    den = lam_re * lam_re + lam_im * lam_im
    nr, ni = ab_re - 1.0, ab_im
    zr = (nr * lam_re + ni * lam_im) / den
    zi = (ni * lam_re - nr * lam_im) / den
    bb_re = zr[..., None] * b_re - zi[..., None] * b_im
    bb_im = zr[..., None] * b_im + zi[..., None] * b_re
    eye = jnp.eye(gpt, dtype=F32)

    def in_map(bb):
        bb = bb.reshape(nt, gpt, p, SSM_GROUP)
        m = jnp.einsum("lgpc,gh->lgchp", bb, eye)
        return m.reshape(nt, LANES, gpt * p).astype(BF16)

    def out_map(cc):
        cc = cc.reshape(nt, gpt, SSM_GROUP, p)
        m = jnp.einsum("lgcp,gh->lhpgc", cc, eye)
        return m.reshape(nt, gpt * p, LANES).astype(BF16)

    return (in_map(bb_re), in_map(bb_im), ab_re.reshape(1, g * p), ab_im.reshape(1, g * p),
            out_map(c_re), out_map(c_im))


def _s5_mixer(proj_a, params, d_skip, glu_w, glu_b, gain, bsz, tp):
    tb = SEQ_BLOCK
    b_re, b_im, a_re, a_im, c_re, c_im = params
    nt, _, sw = b_re.shape
    width = nt * LANES
    pa = proj_a.reshape(bsz, tp, proj_a.shape[-1])
    const = lambda shp: _resident(shp, lambda b, t: (0,) * len(shp))
    return pl.pallas_call(
        _s5_kernel, grid=(bsz, tp // tb),
        in_specs=[pl.BlockSpec((None, tb, width), lambda b, t: (b, t, 0)),
                  const(b_re.shape), const(b_im.shape), const(a_re.shape), const(a_im.shape),
                  const(c_re.shape), const(c_im.shape), const((1, width)), const((width, width)),
                  const((1, width)), const((1, width))],
        out_specs=pl.BlockSpec((None, tb, width), lambda b, t: (b, t, 0)),
        out_shape=jax.ShapeDtypeStruct((bsz, tp, width), BF16),
        scratch_shapes=[pltpu.VMEM((tb, nt * sw), F32), pltpu.VMEM((tb, nt * sw), F32),
                        pltpu.VMEM((SUBLANES, nt * sw), F32), pltpu.VMEM((SUBLANES, nt * sw), F32)],
        compiler_params=_cparams(("arbitrary", "arbitrary")), name="s5_mixer",
    )(pa, b_re, b_im, a_re, a_im, c_re, c_im, d_skip.reshape(1, width), glu_w.astype(BF16),
      glu_b.reshape(1, width), gain.reshape(1, width))


def _out_proj_kernel(alpha, ys_ref, yf_ref, yd_ref, w_ref, h_ref, g_ref, b_ref, o32_ref, o16_ref):
    n1 = ys_ref.shape[1]
    n2 = n1 + yf_ref.shape[1]
    m = (jnp.dot(ys_ref[...], w_ref[0:n1, :], preferred_element_type=F32)
         + jnp.dot(yf_ref[...], w_ref[n1:n2, :], preferred_element_type=F32)
         + jnp.dot(yd_ref[...], w_ref[n2:, :], preferred_element_type=F32))
    y = _ln_rows(alpha * h_ref[...] + m, g_ref[...], b_ref[...])
    o32_ref[...] = y
    o16_ref[...] = y.astype(BF16)


def _out_proj(y_ssm, y_fox, y_dsa, w_out, h, g, b, alpha):
    m, d = h.shape
    tm = _pick(m, (256, 128))
    row = lambda w: pl.BlockSpec((tm, w), lambda i: (i, 0))
    vec = pl.BlockSpec((1, d), lambda i: (0, 0))
    return pl.pallas_call(
        functools.partial(_out_proj_kernel, alpha), grid=(m // tm,),
        in_specs=[row(y_ssm.shape[1]), row(y_fox.shape[1]), row(y_dsa.shape[1]),
                  _resident((d, d), lambda i: (0, 0)), row(d), vec, vec],
        out_specs=(row(d), row(d)),
        out_shape=(jax.ShapeDtypeStruct((m, d), F32), jax.ShapeDtypeStruct((m, d), BF16)),
        compiler_params=_cparams(("parallel",)), name="out_proj",
    )(y_ssm, y_fox, y_dsa, w_out.astype(BF16), h, g.reshape(1, d), b.reshape(1, d))


HALO = 16


def _ffn_kernel(h_ref, halo_ref, wv_ref, wg_ref, cwv_ref, cwg_ref, cbv_ref, cbg_ref, wd_ref, o_ref):
    ti = pl.program_id(1)
    j = pl.program_id(2)
    h = h_ref[...]
    halo = halo_ref[...]
    tm = h.shape[0]
    first = ti == 0

    def conv(w_ref, cw_ref, cb_ref):
        a = jnp.dot(h, w_ref[...], preferred_element_type=F32)
        row = lax.broadcasted_iota(I32, a.shape, 0)
        ah = jnp.dot(halo, w_ref[...], preferred_element_type=F32)
        ah = jnp.where(first, 0.0, ah)
        p1 = jnp.where(row == 0, ah[HALO - 1:HALO, :], pltpu.roll(a, 1, axis=0))
        p2 = jnp.where(row == 0, ah[HALO - 2:HALO - 1, :],
                       jnp.where(row == 1, ah[HALO - 1:HALO, :], pltpu.roll(a, 2, axis=0)))
        cw = cw_ref[...]
        return cw[0:1, :] * p2 + cw[1:2, :] * p1 + cw[2:3, :] * a + cb_ref[...]

    val = conv(wv_ref, cwv_ref, cbv_ref)
    gate = conv(wg_ref, cwg_ref, cbg_ref)
    act = (jax.nn.silu(gate) * val).astype(BF16)
    contrib = jnp.dot(act, wd_ref[...], preferred_element_type=F32)

    @pl.when(j == 0)
    def _():
        o_ref[...] = contrib

    @pl.when(j > 0)
    def _():
        o_ref[...] += contrib


def _conv_ffn(h16, w_up, conv_w, conv_b, w_down, bsz, tp):
    d = h16.shape[-1]
    dff = w_down.shape[0]
    tm = _pick(tp, (768, 512, 256))
    tf = _pick(dff, (512, 256, 128))
    nf = dff // tf
    hb = h16.reshape(bsz, tp, d)
    w_up16 = w_up.astype(BF16)
    cw8 = jnp.zeros((SUBLANES, 2 * dff), F32).at[:CONV_WIDTH].set(conv_w)
    cb = conv_b.reshape(1, 2 * dff)
    halo_blocks = tm // HALO
    return pl.pallas_call(
        _ffn_kernel, grid=(bsz, tp // tm, nf),
        in_specs=[pl.BlockSpec((None, tm, d), lambda b, i, j: (b, i, 0)),
                  pl.BlockSpec((None, HALO, d), lambda b, i, j: (b, jnp.maximum(i * halo_blocks - 1, 0), 0)),
                  pl.BlockSpec((d, tf), lambda b, i, j: (0, j)),
                  pl.BlockSpec((d, tf), lambda b, i, j: (0, j + nf)),
                  pl.BlockSpec((SUBLANES, tf), lambda b, i, j: (0, j)),
                  pl.BlockSpec((SUBLANES, tf), lambda b, i, j: (0, j + nf)),
                  pl.BlockSpec((1, tf), lambda b, i, j: (0, j)),
                  pl.BlockSpec((1, tf), lambda b, i, j: (0, j + nf)),
                  pl.BlockSpec((tf, d), lambda b, i, j: (j, 0))],
        out_specs=pl.BlockSpec((None, tm, d), lambda b, i, j: (b, i, 0)),
        out_shape=jax.ShapeDtypeStruct((bsz, tp, d), F32),
        compiler_params=_cparams(("parallel", "parallel", "arbitrary")), name="conv_ffn",
    )(hb, hb, w_up16, w_up16, cw8, cw8, cb, cb, w_down.astype(BF16)).reshape(bsz * tp, d)


def _split_w_in(w_in, d_model):
    ssm_w = d_model // 2
    fox_w = d_model // 4
    dsa_w = d_model - ssm_w - fox_w
    kv_rank = d_model // 16
    fox_h = fox_w // HEAD_DIM
    sizes = (ssm_w, fox_w, fox_w, fox_w, fox_h, dsa_w, kv_rank, IDX_HEADS * IDX_DIM, IDX_DIM, IDX_HEADS)
    offs = np.cumsum((0,) + sizes)
    seg = lambda k: w_in[:, offs[k]:offs[k + 1]]
    u, fq, fk, fv, ff, dq, ckv, iq, ik, iw = (seg(k) for k in range(10))
    w_a = jnp.concatenate([u, fq, fk, fv, dq, iq, ckv], axis=1).astype(BF16)
    pad = jnp.zeros((w_in.shape[0], LANES - IDX_DIM - fox_h - IDX_HEADS), w_in.dtype)
    w_b = jnp.concatenate([ik, ff, iw, pad], axis=1).astype(BF16)
    return w_a, w_b


def _mixer(h16, h32, lp, cos_t, sin_t, topk, bsz, tp, alpha):
    d = h32.shape[-1]
    w_a, w_b = _split_w_in(lp["w_in"], d)
    proj_a = _matmul(h16, w_a, BF16)
    proj_b = _matmul(h16, w_b, F32)
    q_aug, k_aug, fv_t, q_pad, k_pair, dv_t, iq_pad, ik2, iw_t = _prep(
        proj_a, proj_b, cos_t, sin_t, lp["kv_norm_g"], lp["w_kv_up"], lp["fox_f_bias"], bsz, tp)
    ssm_w = d // 2
    fox_w = d // 4
    g_ssm, g_fox, g_dsa = jnp.split(lp["mix_norm_g"], (ssm_w, ssm_w + fox_w))
    gain = lambda g: jnp.broadcast_to(g[:, None], (g.shape[0], SEQ_BLOCK))

    y_ssm = _s5_mixer(proj_a, _s5_params(lp["ssm_lam_re"], lp["ssm_lam_im"], lp["ssm_log_dt"], lp["ssm_b_re"],
                                         lp["ssm_b_im"], lp["ssm_c_re"], lp["ssm_c_im"]),
                      lp["ssm_d"], lp["ssm_glu_w"], lp["ssm_glu_b"], g_ssm, bsz, tp)
    y_fox = _fox_attention(q_aug, k_aug, fv_t, gain(g_fox), bsz, tp)
    y_dsa = _dsa_attention(iq_pad, iw_t, q_pad, ik2, k_pair, dv_t, gain(g_dsa), topk, bsz, tp)
    return _out_proj(y_ssm.reshape(bsz * tp, -1), y_fox.reshape(bsz * tp, -1), y_dsa.reshape(bsz * tp, -1),
                     lp["w_out"], h32, lp["ln1_g"], lp["ln1_b"], alpha)


def kernel(x, meta_tokens, ln_in_g, ln_in_b, w_in, ssm_lam_re, ssm_lam_im, ssm_log_dt, ssm_b_re, ssm_b_im,
           ssm_c_re, ssm_c_im, ssm_d, ssm_glu_w, ssm_glu_b, fox_f_bias, kv_norm_g, w_kv_up, mix_norm_g, w_out,
           ln1_g, ln1_b, w_up, conv_w, conv_b, w_down, ln2_g, ln2_b):
    bsz, seq, d = x.shape
    depth = w_in.shape[0]
    alpha = (2.0 * depth) ** 0.25
    topk = min(TOPK_MAX, seq // 4)
    t_real = seq + N_META
    tp = -(-t_real // SEQ_BLOCK) * SEQ_BLOCK

    meta = jnp.broadcast_to(meta_tokens[None].astype(x.dtype), (bsz, N_META, d))
    xcat = jnp.concatenate([meta, x, jnp.zeros((bsz, tp - t_real, d), x.dtype)], axis=1)
    h32, h16 = _layer_norm(xcat.reshape(bsz * tp, d), ln_in_g, ln_in_b)
    cos_t, sin_t = _rope_tables(tp)

    stacked = dict(w_in=w_in, ssm_lam_re=ssm_lam_re, ssm_lam_im=ssm_lam_im, ssm_log_dt=ssm_log_dt,
                   ssm_b_re=ssm_b_re, ssm_b_im=ssm_b_im, ssm_c_re=ssm_c_re, ssm_c_im=ssm_c_im, ssm_d=ssm_d,
                   ssm_glu_w=ssm_glu_w, ssm_glu_b=ssm_glu_b, fox_f_bias=fox_f_bias, kv_norm_g=kv_norm_g,
                   w_kv_up=w_kv_up, mix_norm_g=mix_norm_g, w_out=w_out, ln1_g=ln1_g, ln1_b=ln1_b, w_up=w_up,
                   conv_w=conv_w, conv_b=conv_b, w_down=w_down, ln2_g=ln2_g, ln2_b=ln2_b)
    for l in range(depth):
        lp = {k: v[l] for k, v in stacked.items()}
        h32, h16 = _mixer(h16, h32, lp, cos_t, sin_t, topk, bsz, tp, alpha)
        f = _conv_ffn(h16, lp["w_up"], lp["conv_w"], lp["conv_b"], lp["w_down"], bsz, tp)
        h32, h16 = _layer_norm(h32, lp["ln2_g"], lp["ln2_b"], f=f, alpha=alpha)
    return h32.reshape(bsz, tp, d)[:, N_META:t_real, :]
```

```python
import functools
import math

import jax
import jax.numpy as jnp
import numpy as np
from jax import lax
from jax.experimental import pallas as pl
from jax.experimental.pallas import tpu as pltpu

F32 = jnp.float32
BF16 = jnp.bfloat16
I32 = jnp.int32
I16 = jnp.int16
HALF_BIAS = 1 << 15

N_META = 16
HEAD_DIM = 64
SSM_GROUP = 16
SSM_STATE = 64
IDX_HEADS = 16
IDX_DIM = 64
TOPK_MAX = 256
ROPE_THETA = 500000.0
ROT_DIM = HEAD_DIM // 4
CONV_WIDTH = 3
LN_EPS = 1e-5
RMS_EPS = 1e-6
NEG_INF = -1e30

LANES = 128
SUBLANES = 8
VMEM_LIMIT_BYTES = 56 * 1024 * 1024

SEQ_BLOCK = 256
AUG_ROWS = 80
INT_MIN = -(2 ** 31)
SCAN_UNROLL = 4


def _pick(n, candidates):
    for c in candidates:
        if n % c == 0:
            return c
    raise ValueError(f"no tile for {n} in {candidates}")


def _cparams(sem, vmem=VMEM_LIMIT_BYTES):
    return pltpu.CompilerParams(dimension_semantics=sem, vmem_limit_bytes=vmem)


def _resident(shape, index_map):
    return pl.BlockSpec(shape, index_map, pipeline_mode=pl.Buffered(1))


def _ln_rows(z, g, b):
    mu = jnp.mean(z, axis=-1, keepdims=True)
    zc = z - mu
    var = jnp.mean(zc * zc, axis=-1, keepdims=True)
    return zc * lax.rsqrt(var + LN_EPS) * g + b


def _ln_kernel(x_ref, g_ref, b_ref, o32_ref, o16_ref):
    y = _ln_rows(x_ref[...], g_ref[...], b_ref[...])
    o32_ref[...] = y
    o16_ref[...] = y.astype(BF16)


def _layer_norm(x, g, b):
    m, d = x.shape
    tm = _pick(m, (512, 256, 128))
    row = pl.BlockSpec((tm, d), lambda i: (i, 0))
    vec = pl.BlockSpec((1, d), lambda i: (0, 0))
    outs = (jax.ShapeDtypeStruct((m, d), F32), jax.ShapeDtypeStruct((m, d), BF16))
    return pl.pallas_call(
        _ln_kernel, grid=(m // tm,), in_specs=[row, vec, vec], out_specs=(row, row),
        out_shape=outs, compiler_params=_cparams(("parallel",)), name="layer_norm",
    )(x, g.reshape(1, d), b.reshape(1, d))


def _mm_kernel(a_ref, w_ref, o_ref):
    o_ref[...] = jnp.dot(a_ref[...], w_ref[...], preferred_element_type=F32).astype(o_ref.dtype)


def _matmul(a, w, out_dtype):
    m, k = a.shape
    n = w.shape[1]
    tm = _pick(m, (1536, 1024, 768, 512, 256))
    tn = _pick(n, (1408, 1024, 512, 256, 128))
    return pl.pallas_call(
        _mm_kernel, grid=(m // tm, n // tn),
        in_specs=[pl.BlockSpec((tm, k), lambda i, j: (i, 0)), pl.BlockSpec((k, tn), lambda i, j: (0, j))],
        out_specs=pl.BlockSpec((tm, tn), lambda i, j: (i, j)),
        out_shape=jax.ShapeDtypeStruct((m, n), out_dtype),
        compiler_params=_cparams(("parallel", "parallel")), name="matmul",
    )(a, w)


def _rope_perm():
    p = np.zeros((LANES, LANES), np.float32)
    half = ROT_DIM // 2
    for c in range(LANES):
        cc = c % HEAD_DIM
        if cc < half:
            p[c + half, c] = -1.0
        elif cc < ROT_DIM:
            p[c - half, c] = 1.0
    return jnp.asarray(p, BF16)


def _rope_tables(tp):
    pos = jnp.arange(tp, dtype=F32)
    inv_freq = ROPE_THETA ** (-jnp.arange(0, ROT_DIM, 2, dtype=F32) / ROT_DIM)
    ang = pos[:, None] * inv_freq[None, :]
    cos, sin = jnp.cos(ang), jnp.sin(ang)
    half = ROT_DIM // 2
    lane = np.arange(LANES) % HEAD_DIM
    rot = jnp.asarray(lane < ROT_DIM)
    idx = jnp.asarray(lane % half)
    cos_t = jnp.where(rot[None, :], cos[:, idx], 1.0)
    sin_t = jnp.where(rot[None, :], sin[:, idx], 0.0)
    return cos_t, sin_t


def _split3(x):
    hi = x.astype(BF16)
    r1 = x - hi.astype(F32)
    mid = r1.astype(BF16)
    lo = (r1 - mid.astype(F32)).astype(BF16)
    return hi, mid, lo


def _placement_constants(n_fox_heads):
    nh = n_fox_heads
    sq = np.zeros((nh * HEAD_DIM, nh * LANES), np.float32)
    for h in range(nh):
        for d in range(HEAD_DIM):
            sq[h * HEAD_DIM + d, h * LANES + d] = 1.0
    eq = np.zeros((3 * LANES, nh * LANES), np.float32)
    ek = np.zeros((3 * LANES, nh * LANES), np.float32)
    cq = np.zeros((1, nh * LANES), np.float32)
    ck = np.zeros((1, nh * LANES), np.float32)
    for h in range(nh):
        for j in range(3):
            eq[j * LANES + IDX_DIM + h, h * LANES + HEAD_DIM + j] = 1.0
            ek[j * LANES + IDX_DIM + h, h * LANES + HEAD_DIM + 3 + j] = -1.0
            cq[0, h * LANES + HEAD_DIM + 3 + j] = 1.0
            ck[0, h * LANES + HEAD_DIM + j] = 1.0
    dup = np.zeros((LANES, LANES), np.float32)
    for d in range(IDX_DIM):
        dup[d, d] = 1.0
        dup[d, d + IDX_DIM] = 1.0
    bf = lambda a: jnp.asarray(a, BF16)
    return bf(sq * HEAD_DIM ** -0.5), bf(sq), bf(eq), bf(ek), jnp.asarray(cq), jnp.asarray(ck), bf(dup)


def _prep_kernel(fq_ref, fk_ref, fv_ref, dq_ref, iq_ref, ckv_ref, misc_ref, cos_ref, sin_ref, perm_ref,
                 kvg_ref, wkv_ref, fb_ref, sq_ref, sk_ref, eq_ref, ek_ref, cq_ref, ck_ref, dup_ref,
                 qa_o, ka_o, fvt_o, dqp_o, dkp_o, dvt_o, iqp_o, ik2_o, iwt_o, carry_ref):
    t = pl.program_id(1)
    cos = cos_ref[...]
    sin = sin_ref[...]
    perm = perm_ref[...]
    tb = cos.shape[0]
    lane = lax.broadcasted_iota(I32, (tb, LANES), 1)
    lower = lane < HEAD_DIM

    def rope_tiles(x):
        outs = []
        for j in range(x.shape[1] // LANES):
            xt = x[:, j * LANES:(j + 1) * LANES]
            yt = jnp.dot(xt.astype(BF16), perm, preferred_element_type=F32)
            outs.append(xt * cos + yt * sin)
        return outs

    def pair_split(tile, scale, out_ref, p):
        v = (tile * scale).astype(BF16)
        zero = jnp.zeros_like(v)
        out_ref[2 * p] = jnp.where(lower, v, zero)
        out_ref[2 * p + 1] = jnp.where(lower, zero, v)

    def values_t(v, out_ref):
        sub = lax.broadcasted_iota(I32, (AUG_ROWS - HEAD_DIM, tb), 0)
        tail = jnp.where(sub == 0, 1.0, 0.0).astype(BF16)
        for p in range(v.shape[1] // LANES):
            vt = v[:, p * LANES:(p + 1) * LANES].astype(F32).T
            for q in range(2):
                out_ref[2 * p + q, 0:HEAD_DIM, :] = vt[q * HEAD_DIM:(q + 1) * HEAD_DIM, :].astype(BF16)
                out_ref[2 * p + q, HEAD_DIM:AUG_ROWS, :] = tail

    for p, tile in enumerate(rope_tiles(dq_ref[...].astype(F32))):
        pair_split(tile, HEAD_DIM ** -0.5, dqp_o, p)
    for p, tile in enumerate(rope_tiles(iq_ref[...].astype(F32))):
        pair_split(tile, IDX_DIM ** -0.5, iqp_o, p)
    ckv = ckv_ref[...].astype(F32)
    ckvn = ckv * lax.rsqrt(jnp.mean(ckv * ckv, axis=-1, keepdims=True) + RMS_EPS) * kvg_ref[...]
    kv = jnp.dot(ckvn.astype(BF16), wkv_ref[...], preferred_element_type=F32)
    half = kv.shape[1] // 2
    for p, tile in enumerate(rope_tiles(kv[:, :half])):
        dkp_o[p] = tile.astype(BF16)
    values_t(kv[:, half:].astype(BF16), dvt_o)
    values_t(fv_ref[...], fvt_o)

    x = misc_ref[...]
    ik = rope_tiles(x)[0].astype(BF16)
    ik2_o[...] = jnp.dot(ik, dup_ref[...], preferred_element_type=F32).astype(BF16)
    nfh = fq_ref.shape[1] // HEAD_DIM
    iwt_o[...] = (x * (IDX_HEADS ** -0.5)).T[IDX_DIM + nfh:IDX_DIM + nfh + IDX_HEADS, :]
    logf = jax.nn.log_sigmoid(x + fb_ref[...])
    logf = jnp.where((lane >= IDX_DIM) & (lane < IDX_DIM + nfh), logf, 0.0)

    @pl.when(t == 0)
    def _():
        carry_ref[...] = jnp.zeros_like(carry_ref)

    r = lax.broadcasted_iota(I32, (tb, tb), 0)
    c = lax.broadcasted_iota(I32, (tb, tb), 1)
    tril = jnp.where(r >= c, 1.0, 0.0).astype(BF16)
    hi, mid, lo = _split3(logf)
    cum = (jnp.dot(tril, hi, preferred_element_type=F32) + jnp.dot(tril, mid, preferred_element_type=F32)
           + jnp.dot(tril, lo, preferred_element_type=F32)) + carry_ref[0:1, :]
    carry_ref[0:1, :] = cum[tb - 1:tb, :]

    cum3 = jnp.concatenate(_split3(cum), axis=1)
    qa = (jnp.dot(fq_ref[...], sq_ref[...], preferred_element_type=F32)
          + jnp.dot(cum3, eq_ref[...], preferred_element_type=F32) + cq_ref[...]).astype(BF16)
    ka = (jnp.dot(fk_ref[...], sk_ref[...], preferred_element_type=F32)
          + jnp.dot(cum3, ek_ref[...], preferred_element_type=F32) + ck_ref[...]).astype(BF16)
    for h in range(nfh):
        qa_o[h] = qa[:, h * LANES:(h + 1) * LANES]
        ka_o[h] = ka[:, h * LANES:(h + 1) * LANES]


def _prep(proj_a, proj_b, cos_t, sin_t, kv_norm_g, w_kv_up, f_bias, bsz, tp):
    tb = SEQ_BLOCK
    na = proj_a.shape[-1]
    pa = proj_a.reshape(bsz, tp, na)
    pb = proj_b.reshape(bsz, tp, LANES)
    fb = jnp.zeros((1, LANES), F32).at[0, IDX_DIM:IDX_DIM + 8].set(f_bias)
    nfh = f_bias.shape[0]
    ndh = 512 // HEAD_DIM
    consts = _placement_constants(nfh)
    bs = lambda w, cb: pl.BlockSpec((None, tb, w), lambda b, t: (b, t, cb))
    tab = pl.BlockSpec((tb, LANES), lambda b, t: (t, 0))
    const = lambda a: pl.BlockSpec(a.shape, lambda b, t: (0,) * a.ndim)
    heads = lambda n: pl.BlockSpec((None, n, tb, LANES), lambda b, t: (b, 0, t, 0))
    heads_t = lambda n: pl.BlockSpec((None, n, AUG_ROWS, tb), lambda b, t: (b, 0, 0, t))
    sds = jax.ShapeDtypeStruct
    out_shapes = (
        sds((bsz, nfh, tp, LANES), BF16), sds((bsz, nfh, tp, LANES), BF16), sds((bsz, nfh, AUG_ROWS, tp), BF16),
        sds((bsz, ndh, tp, LANES), BF16), sds((bsz, ndh // 2, tp, LANES), BF16), sds((bsz, ndh, AUG_ROWS, tp), BF16),
        sds((bsz, IDX_HEADS, tp, LANES), BF16), sds((bsz, tp, LANES), BF16), sds((bsz, IDX_HEADS, tp), F32),
    )
    out_specs = (heads(nfh), heads(nfh), heads_t(nfh), heads(ndh), heads(ndh // 2), heads_t(ndh),
                 heads(IDX_HEADS), bs(LANES, 0), pl.BlockSpec((None, IDX_HEADS, tb), lambda b, t: (b, 0, t)))
    fixed = (_rope_perm(), kv_norm_g.reshape(1, LANES), w_kv_up.astype(BF16), fb) + consts
    return pl.pallas_call(
        _prep_kernel, grid=(bsz, tp // tb),
        in_specs=[bs(512, 2), bs(512, 3), bs(512, 4), bs(512, 5), bs(1024, 3), bs(LANES, 32), bs(LANES, 0),
                  tab, tab] + [const(a) for a in fixed],
        out_specs=out_specs, out_shape=out_shapes, scratch_shapes=[pltpu.VMEM((SUBLANES, LANES), F32)],
        compiler_params=_cparams(("arbitrary", "arbitrary")), name="prep",
    )(pa, pa, pa, pa, pa, pa, pb, cos_t, sin_t, *fixed)


_NT = (((1,), (1,)), ((), ()))


def _attend(q_ref, k_at, v_at, bias_at, n_full, diag_bias, m_ref, acc_ref):
    nh = q_ref.shape[0]
    m_ref[...] = jnp.full(m_ref.shape, NEG_INF, F32)
    acc_ref[...] = jnp.zeros(acc_ref.shape, F32)

    tq = q_ref.shape[1]

    def step(kb, tk, nsub, masked):
        rows = [pl.ds(pl.multiple_of((kb * nsub + j) * tk, tk), tk) for j in range(nsub)]
        bias = [bias_at(r) for r in rows]
        if masked and diag_bias is not None:
            bias = [diag_bias if b is None else b + diag_bias for b in bias]
        units = [(j, h) for j in range(nsub) for h in range(nh)]
        logits = lambda u: lax.dot_general(k_at(u[1], rows[u[0]]), q_ref[u[1]], _NT,
                                           preferred_element_type=F32)
        ahead = 4
        pending = {n: logits(units[n]) for n in range(min(ahead, len(units)))}
        for n, (j, h) in enumerate(units):
            s = pending.pop(n)
            if n + ahead < len(units):
                pending[n + ahead] = logits(units[n + ahead])
            if bias[j] is not None:
                s = bias[j] + s
            m_old = m_ref[h, 0:1, :]
            m_new = jnp.maximum(m_old, jnp.max(s, axis=0, keepdims=True))
            p = jnp.exp(s - m_new)
            acc_ref[h] = (acc_ref[h] * jnp.exp(m_old - m_new)
                          + jnp.dot(v_at(h, rows[j]), p.astype(BF16), preferred_element_type=F32))
            m_ref[h, 0:1, :] = m_new

    def loop_body(tk, nsub, masked, kb, carry):
        step(kb, tk, nsub, masked)
        return carry

    n4 = lax.shift_right_logical(n_full, 2)
    n2 = lax.shift_right_logical(n_full, 1)
    lax.fori_loop(0, n4, functools.partial(loop_body, 2 * tq, 2, False), 0)
    lax.fori_loop(2 * n4, n2, functools.partial(loop_body, 2 * tq, 1, False), 0)
    lax.fori_loop(2 * n2, n_full, functools.partial(loop_body, tq, 1, False), 0)
    lax.fori_loop(n_full, n_full + 1, functools.partial(loop_body, tq, 1, True), 0)
    return jnp.concatenate([acc_ref[h, :HEAD_DIM, :] / acc_ref[h, HEAD_DIM:HEAD_DIM + 1, :] for h in range(nh)],
                           axis=0)


def _diag_bias(tk, tq):
    r = lax.broadcasted_iota(I32, (tk, tq), 0)
    c = lax.broadcasted_iota(I32, (tk, tq), 1)
    return jnp.where(r <= c, 0.0, NEG_INF)


def _group_norm_t(y_t, gain_t):
    ms = jnp.mean(y_t * y_t, axis=0, keepdims=True)
    return (y_t * lax.rsqrt(ms + RMS_EPS) * gain_t).T.astype(BF16)


def _fox_kernel(q_ref, k_ref, v_ref, g_ref, o_ref, m_ref, acc_ref):
    i = pl.program_id(1)
    tq = q_ref.shape[1]
    y_t = _attend(q_ref, lambda h, rows: k_ref[h, rows, :], lambda h, rows: v_ref[h, :, rows],
                  lambda rows: None, i, _diag_bias(tq, tq), m_ref, acc_ref)
    o_ref[...] = _group_norm_t(y_t, g_ref[...])


def _fox_attention(q_aug, k_aug, v_t, gain_t, bsz, tp):
    tq = SEQ_BLOCK
    nh = q_aug.shape[1]
    return pl.pallas_call(
        _fox_kernel, grid=(bsz, tp // tq),
        in_specs=[pl.BlockSpec((None, nh, tq, LANES), lambda b, i: (b, 0, i, 0)),
                  _resident((None, nh, tp, LANES), lambda b, i: (b, 0, 0, 0)),
                  _resident((None, nh, AUG_ROWS, tp), lambda b, i: (b, 0, 0, 0)),
                  _resident((nh * HEAD_DIM, tq), lambda b, i: (0, 0))],
        out_specs=pl.BlockSpec((None, tq, nh * HEAD_DIM), lambda b, i: (b, i, 0)),
        out_shape=jax.ShapeDtypeStruct((bsz, tp, nh * HEAD_DIM), BF16),
        scratch_shapes=[pltpu.VMEM((nh, SUBLANES, tq), F32), pltpu.VMEM((nh, AUG_ROWS, tq), F32)],
        compiler_params=_cparams(("arbitrary", "arbitrary")), name="fox_attention",
    )(q_aug, k_aug, v_t, gain_t)


def _order_key(x):
    b = lax.bitcast_convert_type(x + 0.0, I32)
    return b ^ ((b >> 31) & 0x7FFFFFFF)


def _dsa_kernel(topk, iq_ref, iw_ref, q_ref, ik_ref, k_ref, v_ref, g_ref, o_ref, key_ref, hi_ref, lo_ref,
                m_ref, acc_ref):
    i = pl.program_id(1)
    tq = q_ref.shape[1]
    tk = tq
    diag_bias = _diag_bias(tk, tq)
    blk = lambda kb: pl.ds(pl.multiple_of(kb * tk, tk), tk)
    heads_per_dot = 4

    def score_block(kb, masked):
        ik = ik_ref[blk(kb), :]
        sc = jnp.zeros((tk, tq), F32)
        for j0 in range(0, IDX_HEADS, heads_per_dot):
            iq = iq_ref[j0:j0 + heads_per_dot].reshape(heads_per_dot * tq, LANES)
            lg = lax.dot_general(ik, iq, _NT, preferred_element_type=F32)
            for j in range(heads_per_dot):
                sc = sc + jnp.maximum(lg[:, j * tq:(j + 1) * tq], 0.0) * iw_ref[j0 + j:j0 + j + 1, :]
        if masked:
            sc = jnp.where(diag_bias == 0.0, sc, NEG_INF)
        key = _order_key(sc)
        key_ref[blk(kb), :] = key
        hi_ref[blk(kb), :] = (key >> 16).astype(I16)
        lo_ref[blk(kb), :] = ((key & 0xFFFF) - HALF_BIAS).astype(I16)

    def score_loop(kb, carry):
        score_block(kb, False)
        return carry

    lax.fori_loop(0, i, score_loop, 0)
    score_block(i, True)
    filler = jnp.full((tk, tq), -HALF_BIAS, I16)
    hi_ref[blk(i + 1), :] = filler
    lo_ref[blk(i + 1), :] = filler

    pack = 2 * SUBLANES
    wide = 2 * tk
    n_wide = lax.shift_right_logical(i + 2, 1)
    wblk = lambda kb: pl.ds(pl.multiple_of(kb * wide, wide), wide)

    def count(ref, pred):
        def body(kb, cnt):
            hit = jnp.where(pred(ref[wblk(kb), :]), jnp.ones((), I16), jnp.zeros((), I16))
            for g in range(wide // pack):
                cnt = cnt + hit[g * pack:(g + 1) * pack, :]
            return cnt
        cnt = lax.fori_loop(0, n_wide, body, jnp.zeros((pack, tq), I16))
        return jnp.sum(cnt.astype(I32), axis=0, keepdims=True)

    def kth_largest_half(ref, rank):
        t_u = jnp.zeros((1, tq), I32)
        for bit in range(15, -1, -1):
            cand_u = t_u | (1 << bit)
            cand = (cand_u - HALF_BIAS).astype(I16)
            t_u = jnp.where(count(ref, lambda x: x >= cand) >= rank, cand_u, t_u)
        return t_u - HALF_BIAS

    top_hi = kth_largest_half(hi_ref, topk)
    top_hi16 = top_hi.astype(I16)
    above = count(hi_ref, lambda x: x > top_hi16)

    def keep_lo(kb, carry):
        lo_ref[wblk(kb), :] = jnp.where(hi_ref[wblk(kb), :] == top_hi16, lo_ref[wblk(kb), :],
                                        jnp.asarray(-HALF_BIAS, I16))
        return carry

    lax.fori_loop(0, n_wide, keep_lo, 0)
    top_lo = kth_largest_half(lo_ref, topk - above)
    thr = top_hi * 65536 + (top_lo + HALF_BIAS)
    floor_key = int(np.array(NEG_INF, np.float32).view(np.int32)) ^ 0x7FFFFFFF
    thr = jnp.maximum(thr, np.int32(floor_key + 1))

    y_t = _attend(q_ref, lambda h, rows: k_ref[h // 2, rows, :], lambda h, rows: v_ref[h, :, rows],
                  lambda rows: jnp.where(key_ref[rows, :] >= thr, 0.0, NEG_INF), i, None, m_ref, acc_ref)
    o_ref[...] = _group_norm_t(y_t, g_ref[...])


def _dsa_attention(iq_hm, iw_t, q_pad, ik, k_pair, v_t, gain_t, topk, bsz, tp):
    tq = SEQ_BLOCK
    nh = q_pad.shape[1]
    return pl.pallas_call(
        functools.partial(_dsa_kernel, topk), grid=(bsz, tp // tq),
        in_specs=[pl.BlockSpec((None, IDX_HEADS, tq, LANES), lambda b, i: (b, 0, i, 0)),
                  pl.BlockSpec((None, IDX_HEADS, tq), lambda b, i: (b, 0, i)),
                  pl.BlockSpec((None, nh, tq, LANES), lambda b, i: (b, 0, i, 0)),
                  _resident((None, tp, LANES), lambda b, i: (b, 0, 0)),
                  _resident((None, nh // 2, tp, LANES), lambda b, i: (b, 0, 0, 0)),
                  _resident((None, nh, AUG_ROWS, tp), lambda b, i: (b, 0, 0, 0)),
                  _resident((nh * HEAD_DIM, tq), lambda b, i: (0, 0))],
        out_specs=pl.BlockSpec((None, tq, nh * HEAD_DIM), lambda b, i: (b, i, 0)),
        out_shape=jax.ShapeDtypeStruct((bsz, tp, nh * HEAD_DIM), BF16),
        scratch_shapes=[pltpu.VMEM((tp, tq), I32), pltpu.VMEM((tp + tq, tq), I16), pltpu.VMEM((tp + tq, tq), I16),
                        pltpu.VMEM((nh, SUBLANES, tq), F32), pltpu.VMEM((nh, AUG_ROWS, tq), F32)],
        compiler_params=_cparams(("arbitrary", "arbitrary")), name="dsa_attention",
    )(iq_hm, iw_t, q_pad, ik, k_pair, v_t, gain_t)


def _s5_kernel(u_ref, bre_ref, bim_ref, are_ref, aim_ref, cre_ref, cim_ref, d_ref, gw_ref, gb_ref, gn_ref,
               o_ref, sre_ref, sim_ref, cre_carry, cim_carry):
    t = pl.program_id(1)
    tb = u_ref.shape[0]
    ntile = bre_ref.shape[0]
    sw = bre_ref.shape[2]

    @pl.when(t == 0)
    def _():
        cre_carry[...] = jnp.zeros_like(cre_carry)
        cim_carry[...] = jnp.zeros_like(cim_carry)

    u = u_ref[...]
    for l in range(ntile):
        ul = u[:, l * LANES:(l + 1) * LANES]
        sre_ref[:, l * sw:(l + 1) * sw] = jnp.dot(ul, bre_ref[l], preferred_element_type=F32)
        sim_ref[:, l * sw:(l + 1) * sw] = jnp.dot(ul, bim_ref[l], preferred_element_type=F32)

    ar = are_ref[...]
    ai = aim_ref[...]

    def body(r, carry):
        sr, si = carry
        row = pl.ds(r, 1)
        nr = ar * sr - ai * si + sre_ref[row, :]
        ni = ar * si + ai * sr + sim_ref[row, :]
        sre_ref[row, :] = nr
        sim_ref[row, :] = ni
        return nr, ni

    sr, si = lax.fori_loop(0, tb, body, (cre_carry[0:1, :], cim_carry[0:1, :]), unroll=SCAN_UNROLL)
    cre_carry[0:1, :] = sr
    cim_carry[0:1, :] = si

    ys = []
    for l in range(ntile):
        s_re = sre_ref[:, l * sw:(l + 1) * sw].astype(BF16)
        s_im = sim_ref[:, l * sw:(l + 1) * sw].astype(BF16)
        ys.append(jnp.dot(s_re, cre_ref[l], preferred_element_type=F32)
                  - jnp.dot(s_im, cim_ref[l], preferred_element_type=F32))
    y = jnp.concatenate(ys, axis=1) + d_ref[...] * u.astype(F32)
    g = jax.nn.gelu(y)
    gate = jax.nn.sigmoid(jnp.dot(g.astype(BF16), gw_ref[...], preferred_element_type=F32) + gb_ref[...])
    out = g * gate
    ms = jnp.mean(out * out, axis=-1, keepdims=True)
    o_ref[...] = (out * lax.rsqrt(ms + RMS_EPS) * gn_ref[...]).astype(BF16)


def _s5_params(lam_re, lam_im, log_dt, b_re, b_im, c_re, c_im):
    g, p = lam_re.shape
    gpt = LANES // SSM_GROUP
    nt = g // gpt
    dt = jnp.exp(log_dt)[:, None]
    mag = jnp.exp(lam_re * dt)
    ab_re, ab_im = mag * jnp.cos(lam_im * dt), mag * jnp.sin(lam_im * dt)
    den = lam_re * lam_re + lam_im * lam_im
    nr, ni = ab_re - 1.0, ab_im
    zr = (nr * lam_re + ni * lam_im) / den
    zi = (ni * lam_re - nr * lam_im) / den
    bb_re = zr[..., None] * b_re - zi[..., None] * b_im
    bb_im = zr[..., None] * b_im + zi[..., None] * b_re
    eye = jnp.eye(gpt, dtype=F32)

    def in_map(bb):
        bb = bb.reshape(nt, gpt, p, SSM_GROUP)
        m = jnp.einsum("lgpc,gh->lgchp", bb, eye)
        return m.reshape(nt, LANES, gpt * p).astype(BF16)

    def out_map(cc):
        cc = cc.reshape(nt, gpt, SSM_GROUP, p)
        m = jnp.einsum("lgcp,gh->lhpgc", cc, eye)
        return m.reshape(nt, gpt * p, LANES).astype(BF16)

    return (in_map(bb_re), in_map(bb_im), ab_re.reshape(1, g * p), ab_im.reshape(1, g * p),
            out_map(c_re), out_map(c_im))


def _s5_mixer(proj_a, params, d_skip, glu_w, glu_b, gain, bsz, tp):
    tb = SEQ_BLOCK
    b_re, b_im, a_re, a_im, c_re, c_im = params
    nt, _, sw = b_re.shape
    width = nt * LANES
    pa = proj_a.reshape(bsz, tp, proj_a.shape[-1])
    const = lambda shp: _resident(shp, lambda b, t: (0,) * len(shp))
    return pl.pallas_call(
        _s5_kernel, grid=(bsz, tp // tb),
        in_specs=[pl.BlockSpec((None, tb, width), lambda b, t: (b, t, 0)),
                  const(b_re.shape), const(b_im.shape), const(a_re.shape), const(a_im.shape),
                  const(c_re.shape), const(c_im.shape), const((1, width)), const((width, width)),
                  const((1, width)), const((1, width))],
        out_specs=pl.BlockSpec((None, tb, width), lambda b, t: (b, t, 0)),
        out_shape=jax.ShapeDtypeStruct((bsz, tp, width), BF16),
        scratch_shapes=[pltpu.VMEM((tb, nt * sw), F32), pltpu.VMEM((tb, nt * sw), F32),
                        pltpu.VMEM((SUBLANES, nt * sw), F32), pltpu.VMEM((SUBLANES, nt * sw), F32)],
        compiler_params=_cparams(("arbitrary", "arbitrary")), name="s5_mixer",
    )(pa, b_re, b_im, a_re, a_im, c_re, c_im, d_skip.reshape(1, width), glu_w.astype(BF16),
      glu_b.reshape(1, width), gain.reshape(1, width))


def _out_proj_kernel(alpha, ys_ref, yf_ref, yd_ref, w_ref, h_ref, g_ref, b_ref, o32_ref, o16_ref):
    n1 = ys_ref.shape[1]
    n2 = n1 + yf_ref.shape[1]
    m = (jnp.dot(ys_ref[...], w_ref[0:n1, :], preferred_element_type=F32)
         + jnp.dot(yf_ref[...], w_ref[n1:n2, :], preferred_element_type=F32)
         + jnp.dot(yd_ref[...], w_ref[n2:, :], preferred_element_type=F32))
    y = _ln_rows(alpha * h_ref[...] + m, g_ref[...], b_ref[...])
    o32_ref[...] = y
    o16_ref[...] = y.astype(BF16)


def _out_proj(y_ssm, y_fox, y_dsa, w_out, h, g, b, alpha):
    m, d = h.shape
    tm = _pick(m, (256, 128))
    row = lambda w: pl.BlockSpec((tm, w), lambda i: (i, 0))
    vec = pl.BlockSpec((1, d), lambda i: (0, 0))
    return pl.pallas_call(
        functools.partial(_out_proj_kernel, alpha), grid=(m // tm,),
        in_specs=[row(y_ssm.shape[1]), row(y_fox.shape[1]), row(y_dsa.shape[1]),
                  _resident((d, d), lambda i: (0, 0)), row(d), vec, vec],
        out_specs=(row(d), row(d)),
        out_shape=(jax.ShapeDtypeStruct((m, d), F32), jax.ShapeDtypeStruct((m, d), BF16)),
        compiler_params=_cparams(("parallel",)), name="out_proj",
    )(y_ssm, y_fox, y_dsa, w_out.astype(BF16), h, g.reshape(1, d), b.reshape(1, d))


HALO = 16


def _ffn_up_kernel(h_ref, halo_ref, wv_ref, wg_ref, cwv_ref, cwg_ref, cbv_ref, cbg_ref, o_ref):
    ti = pl.program_id(1)
    h = h_ref[...]
    halo = halo_ref[...]
    tm = h.shape[0]
    first = ti == 0

    tf = wv_ref.shape[1]
    parts = 2 if tf % (2 * LANES) == 0 else 1
    pw = tf // parts

    def project(w_ref, cols):
        a = jnp.dot(h, w_ref[:, cols], preferred_element_type=F32)
        ah = jnp.dot(halo, w_ref[:, cols], preferred_element_type=F32)
        return a, ah

    def conv(proj, cw_ref, cb_ref, cols):
        a, ah = proj
        row = lax.broadcasted_iota(I32, a.shape, 0)
        ah = jnp.where(first, 0.0, ah)
        p1 = jnp.where(row == 0, ah[HALO - 1:HALO, :], pltpu.roll(a, 1, axis=0))
        p2 = jnp.where(row == 0, ah[HALO - 2:HALO - 1, :],
                       jnp.where(row == 1, ah[HALO - 1:HALO, :], pltpu.roll(a, 2, axis=0)))
        cw = cw_ref[:, cols]
        return cw[0:1, :] * p2 + cw[1:2, :] * p1 + cw[2:3, :] * a + cb_ref[:, cols]

    col = [slice(p * pw, (p + 1) * pw) for p in range(parts)]
    proj = [(project(wv_ref, c), project(wg_ref, c)) for c in col]
    for (pv, pg), c in zip(proj, col):
        val = conv(pv, cwv_ref, cbv_ref, c)
        gate = conv(pg, cwg_ref, cbg_ref, c)
        o_ref[:, c] = (jax.nn.silu(gate) * val).astype(BF16)


def _ffn_down_kernel(alpha, a_ref, w_ref, h_ref, g_ref, b_ref, o32_ref, o16_ref):
    f = jnp.dot(a_ref[...], w_ref[...], preferred_element_type=F32)
    y = _ln_rows(alpha * h_ref[...] + f, g_ref[...], b_ref[...])
    o32_ref[...] = y
    o16_ref[...] = y.astype(BF16)


def _conv_ffn(h16, h32, w_up, conv_w, conv_b, w_down, g, b, alpha, bsz, tp):
    d = h16.shape[-1]
    dff = w_down.shape[0]
    tm = _pick(tp, (768, 512, 256))
    tf = _pick(dff, (512, 256, 128))
    nf = dff // tf
    hb = h16.reshape(bsz, tp, d)
    w_up16 = w_up.astype(BF16)
    cw8 = jnp.zeros((SUBLANES, 2 * dff), F32).at[:CONV_WIDTH].set(conv_w)
    cb = conv_b.reshape(1, 2 * dff)
    halo_blocks = tm // HALO
    act = pl.pallas_call(
        _ffn_up_kernel, grid=(bsz, tp // tm, nf),
        in_specs=[pl.BlockSpec((None, tm, d), lambda b, i, j: (b, i, 0)),
                  pl.BlockSpec((None, HALO, d), lambda b, i, j: (b, jnp.maximum(i * halo_blocks - 1, 0), 0)),
                  pl.BlockSpec((d, tf), lambda b, i, j: (0, j)),
                  pl.BlockSpec((d, tf), lambda b, i, j: (0, j + nf)),
                  pl.BlockSpec((SUBLANES, tf), lambda b, i, j: (0, j)),
                  pl.BlockSpec((SUBLANES, tf), lambda b, i, j: (0, j + nf)),
                  pl.BlockSpec((1, tf), lambda b, i, j: (0, j)),
                  pl.BlockSpec((1, tf), lambda b, i, j: (0, j + nf))],
        out_specs=pl.BlockSpec((None, tm, tf), lambda b, i, j: (b, i, j)),
        out_shape=jax.ShapeDtypeStruct((bsz, tp, dff), BF16),
        compiler_params=_cparams(("parallel", "parallel", "parallel")), name="ffn_up_conv",
    )(hb, hb, w_up16, w_up16, cw8, cw8, cb, cb)

    m = bsz * tp
    tr = _pick(m, (384, 256, 128))
    row = lambda w: pl.BlockSpec((tr, w), lambda i: (i, 0))
    vec = pl.BlockSpec((1, d), lambda i: (0, 0))
    return pl.pallas_call(
        functools.partial(_ffn_down_kernel, alpha), grid=(m // tr,),
        in_specs=[row(dff), _resident((dff, d), lambda i: (0, 0)), row(d), vec, vec],
        out_specs=(row(d), row(d)),
        out_shape=(jax.ShapeDtypeStruct((m, d), F32), jax.ShapeDtypeStruct((m, d), BF16)),
        compiler_params=_cparams(("parallel",)), name="ffn_down_norm",
    )(act.reshape(m, dff), w_down.astype(BF16), h32, g.reshape(1, d), b.reshape(1, d))


def _split_w_in(w_in, d_model):
    ssm_w = d_model // 2
    fox_w = d_model // 4
    dsa_w = d_model - ssm_w - fox_w
    kv_rank = d_model // 16
    fox_h = fox_w // HEAD_DIM
    sizes = (ssm_w, fox_w, fox_w, fox_w, fox_h, dsa_w, kv_rank, IDX_HEADS * IDX_DIM, IDX_DIM, IDX_HEADS)
    offs = np.cumsum((0,) + sizes)
    seg = lambda k: w_in[:, offs[k]:offs[k + 1]]
    u, fq, fk, fv, ff, dq, ckv, iq, ik, iw = (seg(k) for k in range(10))
    w_a = jnp.concatenate([u, fq, fk, fv, dq, iq, ckv], axis=1).astype(BF16)
    pad = jnp.zeros((w_in.shape[0], LANES - IDX_DIM - fox_h - IDX_HEADS), w_in.dtype)
    w_b = jnp.concatenate([ik, ff, iw, pad], axis=1).astype(BF16)
    return w_a, w_b


def _mixer(h16, h32, lp, cos_t, sin_t, topk, bsz, tp, alpha):
    d = h32.shape[-1]
    w_a, w_b = _split_w_in(lp["w_in"], d)
    proj_a = _matmul(h16, w_a, BF16)
    proj_b = _matmul(h16, w_b, F32)
    q_aug, k_aug, fv_t, q_pad, k_pair, dv_t, iq_pad, ik2, iw_t = _prep(
        proj_a, proj_b, cos_t, sin_t, lp["kv_norm_g"], lp["w_kv_up"], lp["fox_f_bias"], bsz, tp)
    ssm_w = d // 2
    fox_w = d // 4
    g_ssm, g_fox, g_dsa = jnp.split(lp["mix_norm_g"], (ssm_w, ssm_w + fox_w))
    gain = lambda g: jnp.broadcast_to(g[:, None], (g.shape[0], SEQ_BLOCK))

    y_ssm = _s5_mixer(proj_a, _s5_params(lp["ssm_lam_re"], lp["ssm_lam_im"], lp["ssm_log_dt"], lp["ssm_b_re"],
                                         lp["ssm_b_im"], lp["ssm_c_re"], lp["ssm_c_im"]),
                      lp["ssm_d"], lp["ssm_glu_w"], lp["ssm_glu_b"], g_ssm, bsz, tp)
    y_fox = _fox_attention(q_aug, k_aug, fv_t, gain(g_fox), bsz, tp)
    y_dsa = _dsa_attention(iq_pad, iw_t, q_pad, ik2, k_pair, dv_t, gain(g_dsa), topk, bsz, tp)
    return _out_proj(y_ssm.reshape(bsz * tp, -1), y_fox.reshape(bsz * tp, -1), y_dsa.reshape(bsz * tp, -1),
                     lp["w_out"], h32, lp["ln1_g"], lp["ln1_b"], alpha)


def kernel(x, meta_tokens, ln_in_g, ln_in_b, w_in, ssm_lam_re, ssm_lam_im, ssm_log_dt, ssm_b_re, ssm_b_im,
           ssm_c_re, ssm_c_im, ssm_d, ssm_glu_w, ssm_glu_b, fox_f_bias, kv_norm_g, w_kv_up, mix_norm_g, w_out,
           ln1_g, ln1_b, w_up, conv_w, conv_b, w_down, ln2_g, ln2_b):
    bsz, seq, d = x.shape
    depth = w_in.shape[0]
    alpha = (2.0 * depth) ** 0.25
    topk = min(TOPK_MAX, seq // 4)
    t_real = seq + N_META
    tp = -(-t_real // SEQ_BLOCK) * SEQ_BLOCK

    meta = jnp.broadcast_to(meta_tokens[None].astype(x.dtype), (bsz, N_META, d))
    xcat = jnp.concatenate([meta, x, jnp.zeros((bsz, tp - t_real, d), x.dtype)], axis=1)
    h32, h16 = _layer_norm(xcat.reshape(bsz * tp, d), ln_in_g, ln_in_b)
    cos_t, sin_t = _rope_tables(tp)

    stacked = dict(w_in=w_in, ssm_lam_re=ssm_lam_re, ssm_lam_im=ssm_lam_im, ssm_log_dt=ssm_log_dt,
                   ssm_b_re=ssm_b_re, ssm_b_im=ssm_b_im, ssm_c_re=ssm_c_re, ssm_c_im=ssm_c_im, ssm_d=ssm_d,
                   ssm_glu_w=ssm_glu_w, ssm_glu_b=ssm_glu_b, fox_f_bias=fox_f_bias, kv_norm_g=kv_norm_g,
                   w_kv_up=w_kv_up, mix_norm_g=mix_norm_g, w_out=w_out, ln1_g=ln1_g, ln1_b=ln1_b, w_up=w_up,
                   conv_w=conv_w, conv_b=conv_b, w_down=w_down, ln2_g=ln2_g, ln2_b=ln2_b)
    for l in range(depth):
        lp = {k: v[l] for k, v in stacked.items()}
        h32, h16 = _mixer(h16, h32, lp, cos_t, sin_t, topk, bsz, tp, alpha)
        h32, h16 = _conv_ffn(h16, h32, lp["w_up"], lp["conv_w"], lp["conv_b"], lp["w_down"], lp["ln2_g"],
                             lp["ln2_b"], alpha, bsz, tp)
    return h32.reshape(bsz, tp, d)[:, N_META:t_real, :]
```

```python
import functools
import math

import jax
import jax.numpy as jnp
import numpy as np
from jax import lax
from jax.experimental import pallas as pl
from jax.experimental.pallas import tpu as pltpu

F32 = jnp.float32
BF16 = jnp.bfloat16
I32 = jnp.int32
I16 = jnp.int16
HALF_BIAS = 1 << 15

N_META = 16
HEAD_DIM = 64
SSM_GROUP = 16
SSM_STATE = 64
IDX_HEADS = 16
IDX_DIM = 64
TOPK_MAX = 256
ROPE_THETA = 500000.0
ROT_DIM = HEAD_DIM // 4
CONV_WIDTH = 3
LN_EPS = 1e-5
RMS_EPS = 1e-6
NEG_INF = -1e30

LANES = 128
SUBLANES = 8
VMEM_LIMIT_BYTES = 56 * 1024 * 1024

SEQ_BLOCK = 256
AUG_ROWS = 80
INT_MIN = -(2 ** 31)
SCAN_UNROLL = 4


def _pick(n, candidates):
    for c in candidates:
        if n % c == 0:
            return c
    raise ValueError(f"no tile for {n} in {candidates}")


def _cparams(sem, vmem=VMEM_LIMIT_BYTES):
    return pltpu.CompilerParams(dimension_semantics=sem, vmem_limit_bytes=vmem)


def _resident(shape, index_map):
    return pl.BlockSpec(shape, index_map, pipeline_mode=pl.Buffered(1))


def _ln_rows(z, g, b):
    mu = jnp.mean(z, axis=-1, keepdims=True)
    zc = z - mu
    var = jnp.mean(zc * zc, axis=-1, keepdims=True)
    return zc * lax.rsqrt(var + LN_EPS) * g + b


def _ln_kernel(x_ref, g_ref, b_ref, o32_ref, o16_ref):
    y = _ln_rows(x_ref[...], g_ref[...], b_ref[...])
    o32_ref[...] = y
    o16_ref[...] = y.astype(BF16)


def _layer_norm(x, g, b):
    m, d = x.shape
    tm = _pick(m, (512, 256, 128))
    row = pl.BlockSpec((tm, d), lambda i: (i, 0))
    vec = pl.BlockSpec((1, d), lambda i: (0, 0))
    outs = (jax.ShapeDtypeStruct((m, d), F32), jax.ShapeDtypeStruct((m, d), BF16))
    return pl.pallas_call(
        _ln_kernel, grid=(m // tm,), in_specs=[row, vec, vec], out_specs=(row, row),
        out_shape=outs, compiler_params=_cparams(("parallel",)), name="layer_norm",
    )(x, g.reshape(1, d), b.reshape(1, d))


def _mm_kernel(a_ref, w_ref, o_ref):
    o_ref[...] = jnp.dot(a_ref[...], w_ref[...], preferred_element_type=F32).astype(o_ref.dtype)


def _matmul(a, w, out_dtype):
    m, k = a.shape
    n = w.shape[1]
    tm = _pick(m, (1536, 1024, 768, 512, 256))
    tn = _pick(n, (1408, 1024, 512, 256, 128))
    return pl.pallas_call(
        _mm_kernel, grid=(m // tm, n // tn),
        in_specs=[pl.BlockSpec((tm, k), lambda i, j: (i, 0)), pl.BlockSpec((k, tn), lambda i, j: (0, j))],
        out_specs=pl.BlockSpec((tm, tn), lambda i, j: (i, j)),
        out_shape=jax.ShapeDtypeStruct((m, n), out_dtype),
        compiler_params=_cparams(("parallel", "parallel")), name="matmul",
    )(a, w)


def _rope_perm():
    p = np.zeros((LANES, LANES), np.float32)
    half = ROT_DIM // 2
    for c in range(LANES):
        cc = c % HEAD_DIM
        if cc < half:
            p[c + half, c] = -1.0
        elif cc < ROT_DIM:
            p[c - half, c] = 1.0
    return jnp.asarray(p, BF16)


def _rope_tables(tp):
    pos = jnp.arange(tp, dtype=F32)
    inv_freq = ROPE_THETA ** (-jnp.arange(0, ROT_DIM, 2, dtype=F32) / ROT_DIM)
    ang = pos[:, None] * inv_freq[None, :]
    cos, sin = jnp.cos(ang), jnp.sin(ang)
    half = ROT_DIM // 2
    lane = np.arange(LANES) % HEAD_DIM
    rot = jnp.asarray(lane < ROT_DIM)
    idx = jnp.asarray(lane % half)
    cos_t = jnp.where(rot[None, :], cos[:, idx], 1.0)
    sin_t = jnp.where(rot[None, :], sin[:, idx], 0.0)
    return cos_t, sin_t


def _split3(x):
    hi = x.astype(BF16)
    r1 = x - hi.astype(F32)
    mid = r1.astype(BF16)
    lo = (r1 - mid.astype(F32)).astype(BF16)
    return hi, mid, lo


def _placement_constants(n_fox_heads):
    nh = n_fox_heads
    sq = np.zeros((nh * HEAD_DIM, nh * LANES), np.float32)
    for h in range(nh):
        for d in range(HEAD_DIM):
            sq[h * HEAD_DIM + d, h * LANES + d] = 1.0
    eq = np.zeros((3 * LANES, nh * LANES), np.float32)
    ek = np.zeros((3 * LANES, nh * LANES), np.float32)
    cq = np.zeros((1, nh * LANES), np.float32)
    ck = np.zeros((1, nh * LANES), np.float32)
    for h in range(nh):
        for j in range(3):
            eq[j * LANES + IDX_DIM + h, h * LANES + HEAD_DIM + j] = 1.0
            ek[j * LANES + IDX_DIM + h, h * LANES + HEAD_DIM + 3 + j] = -1.0
            cq[0, h * LANES + HEAD_DIM + 3 + j] = 1.0
            ck[0, h * LANES + HEAD_DIM + j] = 1.0
    dup = np.zeros((LANES, LANES), np.float32)
    for d in range(IDX_DIM):
        dup[d, d] = 1.0
        dup[d, d + IDX_DIM] = 1.0
    bf = lambda a: jnp.asarray(a, BF16)
    return bf(sq * HEAD_DIM ** -0.5), bf(sq), bf(eq), bf(ek), jnp.asarray(cq), jnp.asarray(ck), bf(dup)


def _prep_kernel(fq_ref, fk_ref, fv_ref, dq_ref, iq_ref, ckv_ref, misc_ref, cos_ref, sin_ref, perm_ref,
                 kvg_ref, wkv_ref, fb_ref, sq_ref, sk_ref, eq_ref, ek_ref, cq_ref, ck_ref, dup_ref,
                 qa_o, ka_o, fvt_o, dqp_o, dkp_o, dvt_o, iqp_o, ik2_o, iwt_o, carry_ref):
    t = pl.program_id(1)
    cos = cos_ref[...]
    sin = sin_ref[...]
    perm = perm_ref[...]
    tb = cos.shape[0]
    lane = lax.broadcasted_iota(I32, (tb, LANES), 1)
    lower = lane < HEAD_DIM

    def rope_tiles(x):
        outs = []
        for j in range(x.shape[1] // LANES):
            xt = x[:, j * LANES:(j + 1) * LANES]
            yt = jnp.dot(xt.astype(BF16), perm, preferred_element_type=F32)
            outs.append(xt * cos + yt * sin)
        return outs

    def pair_split(tile, scale, out_ref, p):
        v = (tile * scale).astype(BF16)
        zero = jnp.zeros_like(v)
        out_ref[2 * p] = jnp.where(lower, v, zero)
        out_ref[2 * p + 1] = jnp.where(lower, zero, v)

    def values_t(v, out_ref):
        sub = lax.broadcasted_iota(I32, (AUG_ROWS - HEAD_DIM, tb), 0)
        tail = jnp.where(sub == 0, 1.0, 0.0).astype(BF16)
        for p in range(v.shape[1] // LANES):
            vt = v[:, p * LANES:(p + 1) * LANES].astype(F32).T
            for q in range(2):
                out_ref[2 * p + q, 0:HEAD_DIM, :] = vt[q * HEAD_DIM:(q + 1) * HEAD_DIM, :].astype(BF16)
                out_ref[2 * p + q, HEAD_DIM:AUG_ROWS, :] = tail

    for p, tile in enumerate(rope_tiles(dq_ref[...].astype(F32))):
        pair_split(tile, HEAD_DIM ** -0.5, dqp_o, p)
    for p, tile in enumerate(rope_tiles(iq_ref[...].astype(F32))):
        pair_split(tile, IDX_DIM ** -0.5, iqp_o, p)
    ckv = ckv_ref[...].astype(F32)
    ckvn = ckv * lax.rsqrt(jnp.mean(ckv * ckv, axis=-1, keepdims=True) + RMS_EPS) * kvg_ref[...]
    kv = jnp.dot(ckvn.astype(BF16), wkv_ref[...], preferred_element_type=F32)
    half = kv.shape[1] // 2
    for p, tile in enumerate(rope_tiles(kv[:, :half])):
        dkp_o[p] = tile.astype(BF16)
    values_t(kv[:, half:].astype(BF16), dvt_o)
    values_t(fv_ref[...], fvt_o)

    x = misc_ref[...]
    ik = rope_tiles(x)[0].astype(BF16)
    ik2_o[...] = jnp.dot(ik, dup_ref[...], preferred_element_type=F32).astype(BF16)
    nfh = fq_ref.shape[1] // HEAD_DIM
    iwt_o[...] = (x * (IDX_HEADS ** -0.5)).T[IDX_DIM + nfh:IDX_DIM + nfh + IDX_HEADS, :]
    logf = jax.nn.log_sigmoid(x + fb_ref[...])
    logf = jnp.where((lane >= IDX_DIM) & (lane < IDX_DIM + nfh), logf, 0.0)

    @pl.when(t == 0)
    def _():
        carry_ref[...] = jnp.zeros_like(carry_ref)

    r = lax.broadcasted_iota(I32, (tb, tb), 0)
    c = lax.broadcasted_iota(I32, (tb, tb), 1)
    tril = jnp.where(r >= c, 1.0, 0.0).astype(BF16)
    hi, mid, lo = _split3(logf)
    cum = (jnp.dot(tril, hi, preferred_element_type=F32) + jnp.dot(tril, mid, preferred_element_type=F32)
           + jnp.dot(tril, lo, preferred_element_type=F32)) + carry_ref[0:1, :]
    carry_ref[0:1, :] = cum[tb - 1:tb, :]

    cum3 = jnp.concatenate(_split3(cum), axis=1)
    qa = (jnp.dot(fq_ref[...], sq_ref[...], preferred_element_type=F32)
          + jnp.dot(cum3, eq_ref[...], preferred_element_type=F32) + cq_ref[...]).astype(BF16)
    ka = (jnp.dot(fk_ref[...], sk_ref[...], preferred_element_type=F32)
          + jnp.dot(cum3, ek_ref[...], preferred_element_type=F32) + ck_ref[...]).astype(BF16)
    for h in range(nfh):
        qa_o[h] = qa[:, h * LANES:(h + 1) * LANES]
        ka_o[h] = ka[:, h * LANES:(h + 1) * LANES]


def _prep(proj_a, proj_b, cos_t, sin_t, kv_norm_g, w_kv_up, f_bias, bsz, tp):
    tb = SEQ_BLOCK
    na = proj_a.shape[-1]
    pa = proj_a.reshape(bsz, tp, na)
    pb = proj_b.reshape(bsz, tp, LANES)
    fb = jnp.zeros((1, LANES), F32).at[0, IDX_DIM:IDX_DIM + 8].set(f_bias)
    nfh = f_bias.shape[0]
    ndh = 512 // HEAD_DIM
    consts = _placement_constants(nfh)
    bs = lambda w, cb: pl.BlockSpec((None, tb, w), lambda b, t: (b, t, cb))
    tab = pl.BlockSpec((tb, LANES), lambda b, t: (t, 0))
    const = lambda a: pl.BlockSpec(a.shape, lambda b, t: (0,) * a.ndim)
    heads = lambda n: pl.BlockSpec((None, n, tb, LANES), lambda b, t: (b, 0, t, 0))
    heads_t = lambda n: pl.BlockSpec((None, n, AUG_ROWS, tb), lambda b, t: (b, 0, 0, t))
    sds = jax.ShapeDtypeStruct
    out_shapes = (
        sds((bsz, nfh, tp, LANES), BF16), sds((bsz, nfh, tp, LANES), BF16), sds((bsz, nfh, AUG_ROWS, tp), BF16),
        sds((bsz, ndh, tp, LANES), BF16), sds((bsz, ndh // 2, tp, LANES), BF16), sds((bsz, ndh, AUG_ROWS, tp), BF16),
        sds((bsz, IDX_HEADS, tp, LANES), BF16), sds((bsz, tp, LANES), BF16), sds((bsz, IDX_HEADS, tp), F32),
    )
    out_specs = (heads(nfh), heads(nfh), heads_t(nfh), heads(ndh), heads(ndh // 2), heads_t(ndh),
                 heads(IDX_HEADS), bs(LANES, 0), pl.BlockSpec((None, IDX_HEADS, tb), lambda b, t: (b, 0, t)))
    fixed = (_rope_perm(), kv_norm_g.reshape(1, LANES), w_kv_up.astype(BF16), fb) + consts
    return pl.pallas_call(
        _prep_kernel, grid=(bsz, tp // tb),
        in_specs=[bs(512, 2), bs(512, 3), bs(512, 4), bs(512, 5), bs(1024, 3), bs(LANES, 32), bs(LANES, 0),
                  tab, tab] + [const(a) for a in fixed],
        out_specs=out_specs, out_shape=out_shapes, scratch_shapes=[pltpu.VMEM((SUBLANES, LANES), F32)],
        compiler_params=_cparams(("arbitrary", "arbitrary")), name="prep",
    )(pa, pa, pa, pa, pa, pa, pb, cos_t, sin_t, *fixed)


_NT = (((1,), (1,)), ((), ()))


def _attend(q_ref, k_at, v_at, bias_at, n_full, diag_bias, m_ref, acc_ref):
    nh = q_ref.shape[0]
    m_ref[...] = jnp.full(m_ref.shape, NEG_INF, F32)
    acc_ref[...] = jnp.zeros(acc_ref.shape, F32)

    tq = q_ref.shape[1]

    def step(kb, tk, nsub, masked):
        rows = [pl.ds(pl.multiple_of((kb * nsub + j) * tk, tk), tk) for j in range(nsub)]
        bias = [bias_at(r) for r in rows]
        if masked and diag_bias is not None:
            bias = [diag_bias if b is None else b + diag_bias for b in bias]
        units = [(j, h) for j in range(nsub) for h in range(nh)]
        logits = lambda u: lax.dot_general(k_at(u[1], rows[u[0]]), q_ref[u[1]], _NT,
                                           preferred_element_type=F32)
        ahead = 4
        pending = {n: logits(units[n]) for n in range(min(ahead, len(units)))}
        for n, (j, h) in enumerate(units):
            s = pending.pop(n)
            if n + ahead < len(units):
                pending[n + ahead] = logits(units[n + ahead])
            if bias[j] is not None:
                s = bias[j] + s
            m_old = m_ref[h, 0:1, :]
            m_new = jnp.maximum(m_old, jnp.max(s, axis=0, keepdims=True))
            p = jnp.exp(s - m_new)
            acc_ref[h] = (acc_ref[h] * jnp.exp(m_old - m_new)
                          + jnp.dot(v_at(h, rows[j]), p.astype(BF16), preferred_element_type=F32))
            m_ref[h, 0:1, :] = m_new

    def loop_body(tk, nsub, masked, kb, carry):
        step(kb, tk, nsub, masked)
        return carry

    n4 = lax.shift_right_logical(n_full, 2)
    n2 = lax.shift_right_logical(n_full, 1)
    lax.fori_loop(0, n4, functools.partial(loop_body, 2 * tq, 2, False), 0)
    lax.fori_loop(2 * n4, n2, functools.partial(loop_body, 2 * tq, 1, False), 0)
    lax.fori_loop(2 * n2, n_full, functools.partial(loop_body, tq, 1, False), 0)
    lax.fori_loop(n_full, n_full + 1, functools.partial(loop_body, tq, 1, True), 0)
    return jnp.concatenate([acc_ref[h, :HEAD_DIM, :] / acc_ref[h, HEAD_DIM:HEAD_DIM + 1, :] for h in range(nh)],
                           axis=0)


def _diag_bias(tk, tq):
    r = lax.broadcasted_iota(I32, (tk, tq), 0)
    c = lax.broadcasted_iota(I32, (tk, tq), 1)
    return jnp.where(r <= c, 0.0, NEG_INF)


def _group_norm_t(y_t, gain_t):
    ms = jnp.mean(y_t * y_t, axis=0, keepdims=True)
    return (y_t * lax.rsqrt(ms + RMS_EPS) * gain_t).T.astype(BF16)


def _fox_kernel(q_ref, k_ref, v_ref, g_ref, o_ref, m_ref, acc_ref):
    i = pl.program_id(1)
    tq = q_ref.shape[1]
    y_t = _attend(q_ref, lambda h, rows: k_ref[h, rows, :], lambda h, rows: v_ref[h, :, rows],
                  lambda rows: None, i, _diag_bias(tq, tq), m_ref, acc_ref)
    o_ref[...] = _group_norm_t(y_t, g_ref[...])


def _fox_attention(q_aug, k_aug, v_t, gain_t, bsz, tp):
    tq = SEQ_BLOCK
    nh = q_aug.shape[1]
    return pl.pallas_call(
        _fox_kernel, grid=(bsz, tp // tq),
        in_specs=[pl.BlockSpec((None, nh, tq, LANES), lambda b, i: (b, 0, i, 0)),
                  _resident((None, nh, tp, LANES), lambda b, i: (b, 0, 0, 0)),
                  _resident((None, nh, AUG_ROWS, tp), lambda b, i: (b, 0, 0, 0)),
                  _resident((nh * HEAD_DIM, tq), lambda b, i: (0, 0))],
        out_specs=pl.BlockSpec((None, tq, nh * HEAD_DIM), lambda b, i: (b, i, 0)),
        out_shape=jax.ShapeDtypeStruct((bsz, tp, nh * HEAD_DIM), BF16),
        scratch_shapes=[pltpu.VMEM((nh, SUBLANES, tq), F32), pltpu.VMEM((nh, AUG_ROWS, tq), F32)],
        compiler_params=_cparams(("arbitrary", "arbitrary")), name="fox_attention",
    )(q_aug, k_aug, v_t, gain_t)


def _order_key(x):
    b = lax.bitcast_convert_type(x + 0.0, I32)
    return b ^ ((b >> 31) & 0x7FFFFFFF)


def _dsa_kernel(topk, iq_ref, iw_ref, q_ref, ik_ref, k_ref, v_ref, g_ref, o_ref, key_ref, hi_ref, lo_ref,
                m_ref, acc_ref):
    i = pl.program_id(1)
    tq = q_ref.shape[1]
    tk = tq
    diag_bias = _diag_bias(tk, tq)
    blk = lambda kb: pl.ds(pl.multiple_of(kb * tk, tk), tk)
    heads_per_dot = 4

    def score_rows(kb, rows_n, masked):
        rows = pl.ds(pl.multiple_of(kb * rows_n, rows_n), rows_n)
        ik = ik_ref[rows, :]
        sc = jnp.zeros((rows_n, tq), F32)
        for j0 in range(0, IDX_HEADS, heads_per_dot):
            iq = iq_ref[j0:j0 + heads_per_dot].reshape(heads_per_dot * tq, LANES)
            lg = lax.dot_general(ik, iq, _NT, preferred_element_type=F32)
            for j in range(heads_per_dot):
                sc = sc + jnp.maximum(lg[:, j * tq:(j + 1) * tq], 0.0) * iw_ref[j0 + j:j0 + j + 1, :]
        if masked:
            sc = jnp.where(diag_bias == 0.0, sc, NEG_INF)
        key = _order_key(sc)
        key_ref[rows, :] = key
        hi_ref[rows, :] = (key >> 16).astype(I16)
        lo_ref[rows, :] = ((key & 0xFFFF) - HALF_BIAS).astype(I16)

    def score_loop(rows_n, masked, kb, carry):
        score_rows(kb, rows_n, masked)
        return carry

    n_pair = lax.shift_right_logical(i, 1)
    lax.fori_loop(0, n_pair, functools.partial(score_loop, 2 * tk, False), 0)
    lax.fori_loop(2 * n_pair, i, functools.partial(score_loop, tk, False), 0)
    lax.fori_loop(i, i + 1, functools.partial(score_loop, tk, True), 0)
    filler = jnp.full((tk, tq), -HALF_BIAS, I16)
    hi_ref[blk(i + 1), :] = filler
    lo_ref[blk(i + 1), :] = filler

    pack = 2 * SUBLANES
    wide = 2 * tk
    n_wide = lax.shift_right_logical(i + 2, 1)
    wblk = lambda kb: pl.ds(pl.multiple_of(kb * wide, wide), wide)

    def count(ref, pred):
        def body(kb, cnt):
            hit = jnp.where(pred(ref[wblk(kb), :]), jnp.ones((), I16), jnp.zeros((), I16))
            for g in range(wide // pack):
                cnt = cnt + hit[g * pack:(g + 1) * pack, :]
            return cnt
        cnt = lax.fori_loop(0, n_wide, body, jnp.zeros((pack, tq), I16))
        return jnp.sum(cnt.astype(I32), axis=0, keepdims=True)

    def kth_largest_half(ref, rank):
        t_u = jnp.zeros((1, tq), I32)
        for bit in range(15, -1, -1):
            cand_u = t_u | (1 << bit)
            cand = (cand_u - HALF_BIAS).astype(I16)
            t_u = jnp.where(count(ref, lambda x: x >= cand) >= rank, cand_u, t_u)
        return t_u - HALF_BIAS

    top_hi = kth_largest_half(hi_ref, topk)
    top_hi16 = top_hi.astype(I16)
    above = count(hi_ref, lambda x: x > top_hi16)

    def keep_lo(kb, carry):
        lo_ref[wblk(kb), :] = jnp.where(hi_ref[wblk(kb), :] == top_hi16, lo_ref[wblk(kb), :],
                                        jnp.asarray(-HALF_BIAS, I16))
        return carry

    lax.fori_loop(0, n_wide, keep_lo, 0)
    top_lo = kth_largest_half(lo_ref, topk - above)
    top_lo16 = top_lo.astype(I16)
    at_or_above = above + count(lo_ref, lambda x: x >= top_lo16)
    thr_raw = top_hi * 65536 + (top_lo + HALF_BIAS)
    floor_key = int(np.array(NEG_INF, np.float32).view(np.int32)) ^ 0x7FFFFFFF
    thr = jnp.maximum(thr_raw, np.int32(floor_key + 1))

    surplus = jnp.where(thr_raw == thr, at_or_above - topk, 0)

    @pl.when(jnp.max(surplus) > 0)
    def _():
        def count_rows(pred):
            def body(kb, cnt):
                row = kb * tk + lax.broadcasted_iota(I32, (tk, tq), 0)
                hit = jnp.where(pred(key_ref[blk(kb), :], row), 1, 0).reshape(tk // SUBLANES, SUBLANES, tq)
                return cnt + jnp.sum(hit, axis=0)
            cnt = lax.fori_loop(0, i + 1, body, jnp.zeros((SUBLANES, tq), I32))
            return jnp.sum(cnt, axis=0, keepdims=True)

        keep = topk - count_rows(lambda k, row: k > thr)
        last = jnp.zeros((1, tq), I32)
        for bit in range(int(key_ref.shape[0]).bit_length() - 1, -1, -1):
            cand = last | (1 << bit)
            before = count_rows(lambda k, row: (k == thr) & (row < cand))
            last = jnp.where(before < keep, cand, last)

        def demote(kb, carry):
            row = kb * tk + lax.broadcasted_iota(I32, (tk, tq), 0)
            k = key_ref[blk(kb), :]
            key_ref[blk(kb), :] = jnp.where((k == thr) & (row > last), thr - 1, k)
            return carry

        lax.fori_loop(0, i + 1, demote, 0)

    y_t = _attend(q_ref, lambda h, rows: k_ref[h // 2, rows, :], lambda h, rows: v_ref[h, :, rows],
                  lambda rows: jnp.where(key_ref[rows, :] >= thr, 0.0, NEG_INF), i, None, m_ref, acc_ref)
    o_ref[...] = _group_norm_t(y_t, g_ref[...])


def _dsa_attention(iq_hm, iw_t, q_pad, ik, k_pair, v_t, gain_t, topk, bsz, tp):
    tq = SEQ_BLOCK
    nh = q_pad.shape[1]
    return pl.pallas_call(
        functools.partial(_dsa_kernel, topk), grid=(bsz, tp // tq),
        in_specs=[pl.BlockSpec((None, IDX_HEADS, tq, LANES), lambda b, i: (b, 0, i, 0)),
                  pl.BlockSpec((None, IDX_HEADS, tq), lambda b, i: (b, 0, i)),
                  pl.BlockSpec((None, nh, tq, LANES), lambda b, i: (b, 0, i, 0)),
                  _resident((None, tp, LANES), lambda b, i: (b, 0, 0)),
                  _resident((None, nh // 2, tp, LANES), lambda b, i: (b, 0, 0, 0)),
                  _resident((None, nh, AUG_ROWS, tp), lambda b, i: (b, 0, 0, 0)),
                  _resident((nh * HEAD_DIM, tq), lambda b, i: (0, 0))],
        out_specs=pl.BlockSpec((None, tq, nh * HEAD_DIM), lambda b, i: (b, i, 0)),
        out_shape=jax.ShapeDtypeStruct((bsz, tp, nh * HEAD_DIM), BF16),
        scratch_shapes=[pltpu.VMEM((tp, tq), I32), pltpu.VMEM((tp + tq, tq), I16), pltpu.VMEM((tp + tq, tq), I16),
                        pltpu.VMEM((nh, SUBLANES, tq), F32), pltpu.VMEM((nh, AUG_ROWS, tq), F32)],
        compiler_params=_cparams(("arbitrary", "arbitrary")), name="dsa_attention",
    )(iq_hm, iw_t, q_pad, ik, k_pair, v_t, gain_t)


def _s5_kernel(u_ref, bre_ref, bim_ref, are_ref, aim_ref, cre_ref, cim_ref, d_ref, gw_ref, gb_ref, gn_ref,
               o_ref, sre_ref, sim_ref, cre_carry, cim_carry):
    t = pl.program_id(1)
    tb = u_ref.shape[0]
    ntile = bre_ref.shape[0]
    sw = bre_ref.shape[2]

    @pl.when(t == 0)
    def _():
        cre_carry[...] = jnp.zeros_like(cre_carry)
        cim_carry[...] = jnp.zeros_like(cim_carry)

    u = u_ref[...]
    for l in range(ntile):
        ul = u[:, l * LANES:(l + 1) * LANES]
        sre_ref[:, l * sw:(l + 1) * sw] = jnp.dot(ul, bre_ref[l], preferred_element_type=F32)
        sim_ref[:, l * sw:(l + 1) * sw] = jnp.dot(ul, bim_ref[l], preferred_element_type=F32)

    ar = are_ref[...]
    ai = aim_ref[...]

    def body(r, carry):
        sr, si = carry
        row = pl.ds(r, 1)
        nr = ar * sr - ai * si + sre_ref[row, :]
        ni = ar * si + ai * sr + sim_ref[row, :]
        sre_ref[row, :] = nr
        sim_ref[row, :] = ni
        return nr, ni

    sr, si = lax.fori_loop(0, tb, body, (cre_carry[0:1, :], cim_carry[0:1, :]), unroll=SCAN_UNROLL)
    cre_carry[0:1, :] = sr
    cim_carry[0:1, :] = si

    ys = []
    for l in range(ntile):
        s_re = sre_ref[:, l * sw:(l + 1) * sw].astype(BF16)
        s_im = sim_ref[:, l * sw:(l + 1) * sw].astype(BF16)
        ys.append(jnp.dot(s_re, cre_ref[l], preferred_element_type=F32)
                  - jnp.dot(s_im, cim_ref[l], preferred_element_type=F32))
    y = jnp.concatenate(ys, axis=1) + d_ref[...] * u.astype(F32)
    g = jax.nn.gelu(y)
    gate = jax.nn.sigmoid(jnp.dot(g.astype(BF16), gw_ref[...], preferred_element_type=F32) + gb_ref[...])
    out = g * gate
    ms = jnp.mean(out * out, axis=-1, keepdims=True)
    o_ref[...] = (out * lax.rsqrt(ms + RMS_EPS) * gn_ref[...]).astype(BF16)


def _s5_params(lam_re, lam_im, log_dt, b_re, b_im, c_re, c_im):
    g, p = lam_re.shape
    gpt = LANES // SSM_GROUP
    nt = g // gpt
    dt = jnp.exp(log_dt)[:, None]
    mag = jnp.exp(lam_re * dt)
    ab_re, ab_im = mag * jnp.cos(lam_im * dt), mag * jnp.sin(lam_im * dt)
    den = lam_re * lam_re + lam_im * lam_im
    nr, ni = ab_re - 1.0, ab_im
    zr = (nr * lam_re + ni * lam_im) / den
    zi = (ni * lam_re - nr * lam_im) / den
    bb_re = zr[..., None] * b_re - zi[..., None] * b_im
    bb_im = zr[..., None] * b_im + zi[..., None] * b_re
    eye = jnp.eye(gpt, dtype=F32)

    def in_map(bb):
        bb = bb.reshape(nt, gpt, p, SSM_GROUP)
        m = jnp.einsum("lgpc,gh->lgchp", bb, eye)
        return m.reshape(nt, LANES, gpt * p).astype(BF16)

    def out_map(cc):
        cc = cc.reshape(nt, gpt, SSM_GROUP, p)
        m = jnp.einsum("lgcp,gh->lhpgc", cc, eye)
        return m.reshape(nt, gpt * p, LANES).astype(BF16)

    return (in_map(bb_re), in_map(bb_im), ab_re.reshape(1, g * p), ab_im.reshape(1, g * p),
            out_map(c_re), out_map(c_im))


def _s5_mixer(proj_a, params, d_skip, glu_w, glu_b, gain, bsz, tp):
    tb = SEQ_BLOCK
    b_re, b_im, a_re, a_im, c_re, c_im = params
    nt, _, sw = b_re.shape
    width = nt * LANES
    pa = proj_a.reshape(bsz, tp, proj_a.shape[-1])
    const = lambda shp: _resident(shp, lambda b, t: (0,) * len(shp))
    return pl.pallas_call(
        _s5_kernel, grid=(bsz, tp // tb),
        in_specs=[pl.BlockSpec((None, tb, width), lambda b, t: (b, t, 0)),
                  const(b_re.shape), const(b_im.shape), const(a_re.shape), const(a_im.shape),
                  const(c_re.shape), const(c_im.shape), const((1, width)), const((width, width)),
                  const((1, width)), const((1, width))],
        out_specs=pl.BlockSpec((None, tb, width), lambda b, t: (b, t, 0)),
        out_shape=jax.ShapeDtypeStruct((bsz, tp, width), BF16),
        scratch_shapes=[pltpu.VMEM((tb, nt * sw), F32), pltpu.VMEM((tb, nt * sw), F32),
                        pltpu.VMEM((SUBLANES, nt * sw), F32), pltpu.VMEM((SUBLANES, nt * sw), F32)],
        compiler_params=_cparams(("arbitrary", "arbitrary")), name="s5_mixer",
    )(pa, b_re, b_im, a_re, a_im, c_re, c_im, d_skip.reshape(1, width), glu_w.astype(BF16),
      glu_b.reshape(1, width), gain.reshape(1, width))


def _out_proj_kernel(alpha, ys_ref, yf_ref, yd_ref, w_ref, h_ref, g_ref, b_ref, o32_ref, o16_ref):
    n1 = ys_ref.shape[1]
    n2 = n1 + yf_ref.shape[1]
    m = (jnp.dot(ys_ref[...], w_ref[0:n1, :], preferred_element_type=F32)
         + jnp.dot(yf_ref[...], w_ref[n1:n2, :], preferred_element_type=F32)
         + jnp.dot(yd_ref[...], w_ref[n2:, :], preferred_element_type=F32))
    y = _ln_rows(alpha * h_ref[...] + m, g_ref[...], b_ref[...])
    o32_ref[...] = y
    o16_ref[...] = y.astype(BF16)


def _out_proj(y_ssm, y_fox, y_dsa, w_out, h, g, b, alpha):
    m, d = h.shape
    tm = _pick(m, (256, 128))
    row = lambda w: pl.BlockSpec((tm, w), lambda i: (i, 0))
    vec = pl.BlockSpec((1, d), lambda i: (0, 0))
    return pl.pallas_call(
        functools.partial(_out_proj_kernel, alpha), grid=(m // tm,),
        in_specs=[row(y_ssm.shape[1]), row(y_fox.shape[1]), row(y_dsa.shape[1]),
                  _resident((d, d), lambda i: (0, 0)), row(d), vec, vec],
        out_specs=(row(d), row(d)),
        out_shape=(jax.ShapeDtypeStruct((m, d), F32), jax.ShapeDtypeStruct((m, d), BF16)),
        compiler_params=_cparams(("parallel",)), name="out_proj",
    )(y_ssm, y_fox, y_dsa, w_out.astype(BF16), h, g.reshape(1, d), b.reshape(1, d))


HALO = 16


def _ffn_up_kernel(h_ref, halo_ref, wv_ref, wg_ref, cwv_ref, cwg_ref, cbv_ref, cbg_ref, o_ref, hx_ref):
    ti = pl.program_id(1)
    j = pl.program_id(2)

    @pl.when(j == 0)
    def _():
        hx_ref[0:HALO, :] = jnp.where(ti == 0, jnp.zeros_like(halo_ref[...]), halo_ref[...])
        hx_ref[HALO:, :] = h_ref[...]

    hx = hx_ref[...]
    tf = wv_ref.shape[1]
    parts = 2 if tf % (2 * LANES) == 0 else 1
    pw = tf // parts

    def project(w_ref, cols):
        return jnp.dot(hx, w_ref[:, cols], preferred_element_type=F32)

    def conv(a, cw_ref, cb_ref, cols):
        p1 = pltpu.roll(a, 1, axis=0)
        p2 = pltpu.roll(a, 2, axis=0)
        cw = cw_ref[:, cols]
        y = cw[0:1, :] * p2 + cw[1:2, :] * p1 + cw[2:3, :] * a + cb_ref[:, cols]
        return y[HALO:, :]

    col = [slice(p * pw, (p + 1) * pw) for p in range(parts)]
    proj = [(project(wv_ref, c), project(wg_ref, c)) for c in col]
    for (pv, pg), c in zip(proj, col):
        val = conv(pv, cwv_ref, cbv_ref, c)
        gate = conv(pg, cwg_ref, cbg_ref, c)
        o_ref[:, c] = (jax.nn.silu(gate) * val).astype(BF16)


def _ffn_down_kernel(alpha, a_ref, w_ref, h_ref, g_ref, b_ref, o32_ref, o16_ref):
    f = jnp.dot(a_ref[...], w_ref[...], preferred_element_type=F32)
    y = _ln_rows(alpha * h_ref[...] + f, g_ref[...], b_ref[...])
    o32_ref[...] = y
    o16_ref[...] = y.astype(BF16)


def _conv_ffn(h16, h32, w_up, conv_w, conv_b, w_down, g, b, alpha, bsz, tp):
    d = h16.shape[-1]
    dff = w_down.shape[0]
    tm = _pick(tp, (768, 512, 256))
    tf = _pick(dff, (512, 256, 128))
    nf = dff // tf
    hb = h16.reshape(bsz, tp, d)
    w_up16 = w_up.astype(BF16)
    cw8 = jnp.zeros((SUBLANES, 2 * dff), F32).at[:CONV_WIDTH].set(conv_w)
    cb = conv_b.reshape(1, 2 * dff)
    halo_blocks = tm // HALO
    act = pl.pallas_call(
        _ffn_up_kernel, grid=(bsz, tp // tm, nf),
        in_specs=[pl.BlockSpec((None, tm, d), lambda b, i, j: (b, i, 0)),
                  pl.BlockSpec((None, HALO, d), lambda b, i, j: (b, jnp.maximum(i * halo_blocks - 1, 0), 0)),
                  pl.BlockSpec((d, tf), lambda b, i, j: (0, j)),
                  pl.BlockSpec((d, tf), lambda b, i, j: (0, j + nf)),
                  pl.BlockSpec((SUBLANES, tf), lambda b, i, j: (0, j)),
                  pl.BlockSpec((SUBLANES, tf), lambda b, i, j: (0, j + nf)),
                  pl.BlockSpec((1, tf), lambda b, i, j: (0, j)),
                  pl.BlockSpec((1, tf), lambda b, i, j: (0, j + nf))],
        out_specs=pl.BlockSpec((None, tm, tf), lambda b, i, j: (b, i, j)),
        out_shape=jax.ShapeDtypeStruct((bsz, tp, dff), BF16),
        scratch_shapes=[pltpu.VMEM((HALO + tm, d), BF16)],
        compiler_params=_cparams(("parallel", "parallel", "arbitrary")), name="ffn_up_conv",
    )(hb, hb, w_up16, w_up16, cw8, cw8, cb, cb)

    m = bsz * tp
    tr = _pick(m, (384, 256, 128))
    row = lambda w: pl.BlockSpec((tr, w), lambda i: (i, 0))
    vec = pl.BlockSpec((1, d), lambda i: (0, 0))
    return pl.pallas_call(
        functools.partial(_ffn_down_kernel, alpha), grid=(m // tr,),
        in_specs=[row(dff), _resident((dff, d), lambda i: (0, 0)), row(d), vec, vec],
        out_specs=(row(d), row(d)),
        out_shape=(jax.ShapeDtypeStruct((m, d), F32), jax.ShapeDtypeStruct((m, d), BF16)),
        compiler_params=_cparams(("parallel",)), name="ffn_down_norm",
    )(act.reshape(m, dff), w_down.astype(BF16), h32, g.reshape(1, d), b.reshape(1, d))


def _split_w_in(w_in, d_model):
    ssm_w = d_model // 2
    fox_w = d_model // 4
    dsa_w = d_model - ssm_w - fox_w
    kv_rank = d_model // 16
    fox_h = fox_w // HEAD_DIM
    sizes = (ssm_w, fox_w, fox_w, fox_w, fox_h, dsa_w, kv_rank, IDX_HEADS * IDX_DIM, IDX_DIM, IDX_HEADS)
    offs = np.cumsum((0,) + sizes)
    seg = lambda k: w_in[:, offs[k]:offs[k + 1]]
    u, fq, fk, fv, ff, dq, ckv, iq, ik, iw = (seg(k) for k in range(10))
    w_a = jnp.concatenate([u, fq, fk, fv, dq, iq, ckv], axis=1).astype(BF16)
    pad = jnp.zeros((w_in.shape[0], LANES - IDX_DIM - fox_h - IDX_HEADS), w_in.dtype)
    w_b = jnp.concatenate([ik, ff, iw, pad], axis=1).astype(BF16)
    return w_a, w_b


def _mixer(h16, h32, lp, cos_t, sin_t, topk, bsz, tp, alpha):
    d = h32.shape[-1]
    w_a, w_b = _split_w_in(lp["w_in"], d)
    proj_a = _matmul(h16, w_a, BF16)
    proj_b = _matmul(h16, w_b, F32)
    q_aug, k_aug, fv_t, q_pad, k_pair, dv_t, iq_pad, ik2, iw_t = _prep(
        proj_a, proj_b, cos_t, sin_t, lp["kv_norm_g"], lp["w_kv_up"], lp["fox_f_bias"], bsz, tp)
    ssm_w = d // 2
    fox_w = d // 4
    g_ssm, g_fox, g_dsa = jnp.split(lp["mix_norm_g"], (ssm_w, ssm_w + fox_w))
    gain = lambda g: jnp.broadcast_to(g[:, None], (g.shape[0], SEQ_BLOCK))

    y_ssm = _s5_mixer(proj_a, _s5_params(lp["ssm_lam_re"], lp["ssm_lam_im"], lp["ssm_log_dt"], lp["ssm_b_re"],
                                         lp["ssm_b_im"], lp["ssm_c_re"], lp["ssm_c_im"]),
                      lp["ssm_d"], lp["ssm_glu_w"], lp["ssm_glu_b"], g_ssm, bsz, tp)
    y_fox = _fox_attention(q_aug, k_aug, fv_t, gain(g_fox), bsz, tp)
    y_dsa = _dsa_attention(iq_pad, iw_t, q_pad, ik2, k_pair, dv_t, gain(g_dsa), topk, bsz, tp)
    return _out_proj(y_ssm.reshape(bsz * tp, -1), y_fox.reshape(bsz * tp, -1), y_dsa.reshape(bsz * tp, -1),
                     lp["w_out"], h32, lp["ln1_g"], lp["ln1_b"], alpha)


def kernel(x, meta_tokens, ln_in_g, ln_in_b, w_in, ssm_lam_re, ssm_lam_im, ssm_log_dt, ssm_b_re, ssm_b_im,
           ssm_c_re, ssm_c_im, ssm_d, ssm_glu_w, ssm_glu_b, fox_f_bias, kv_norm_g, w_kv_up, mix_norm_g, w_out,
           ln1_g, ln1_b, w_up, conv_w, conv_b, w_down, ln2_g, ln2_b):
    bsz, seq, d = x.shape
    depth = w_in.shape[0]
    alpha = (2.0 * depth) ** 0.25
    topk = min(TOPK_MAX, seq // 4)
    t_real = seq + N_META
    tp = -(-t_real // SEQ_BLOCK) * SEQ_BLOCK

    meta = jnp.broadcast_to(meta_tokens[None].astype(x.dtype), (bsz, N_META, d))
    xcat = jnp.concatenate([meta, x, jnp.zeros((bsz, tp - t_real, d), x.dtype)], axis=1)
    h32, h16 = _layer_norm(xcat.reshape(bsz * tp, d), ln_in_g, ln_in_b)
    cos_t, sin_t = _rope_tables(tp)

    stacked = dict(w_in=w_in, ssm_lam_re=ssm_lam_re, ssm_lam_im=ssm_lam_im, ssm_log_dt=ssm_log_dt,
                   ssm_b_re=ssm_b_re, ssm_b_im=ssm_b_im, ssm_c_re=ssm_c_re, ssm_c_im=ssm_c_im, ssm_d=ssm_d,
                   ssm_glu_w=ssm_glu_w, ssm_glu_b=ssm_glu_b, fox_f_bias=fox_f_bias, kv_norm_g=kv_norm_g,
                   w_kv_up=w_kv_up, mix_norm_g=mix_norm_g, w_out=w_out, ln1_g=ln1_g, ln1_b=ln1_b, w_up=w_up,
                   conv_w=conv_w, conv_b=conv_b, w_down=w_down, ln2_g=ln2_g, ln2_b=ln2_b)
    for l in range(depth):
        lp = {k: v[l] for k, v in stacked.items()}
        h32, h16 = _mixer(h16, h32, lp, cos_t, sin_t, topk, bsz, tp, alpha)
        h32, h16 = _conv_ffn(h16, h32, lp["w_up"], lp["conv_w"], lp["conv_b"], lp["w_down"], lp["ln2_g"],
                             lp["ln2_b"], alpha, bsz, tp)
    return h32.reshape(bsz, tp, d)[:, N_META:t_real, :]
```

```python
import functools
import math

import jax
import jax.numpy as jnp
import numpy as np
from jax import lax
from jax.experimental import pallas as pl
from jax.experimental.pallas import tpu as pltpu

F32 = jnp.float32
BF16 = jnp.bfloat16
I32 = jnp.int32
I16 = jnp.int16
HALF_BIAS = 1 << 15

N_META = 16
HEAD_DIM = 64
SSM_GROUP = 16
SSM_STATE = 64
IDX_HEADS = 16
IDX_DIM = 64
TOPK_MAX = 256
ROPE_THETA = 500000.0
ROT_DIM = HEAD_DIM // 4
CONV_WIDTH = 3
LN_EPS = 1e-5
RMS_EPS = 1e-6
NEG_INF = -1e30

LANES = 128
SUBLANES = 8
VMEM_LIMIT_BYTES = 56 * 1024 * 1024

SEQ_BLOCK = 256
AUG_ROWS = 80
INT_MIN = -(2 ** 31)
SCAN_UNROLL = 4


def _pick(n, candidates):
    for c in candidates:
        if n % c == 0:
            return c
    raise ValueError(f"no tile for {n} in {candidates}")


def _cparams(sem, vmem=VMEM_LIMIT_BYTES):
    return pltpu.CompilerParams(dimension_semantics=sem, vmem_limit_bytes=vmem)


def _resident(shape, index_map):
    return pl.BlockSpec(shape, index_map, pipeline_mode=pl.Buffered(1))


def _ln_rows(z, g, b):
    mu = jnp.mean(z, axis=-1, keepdims=True)
    zc = z - mu
    var = jnp.mean(zc * zc, axis=-1, keepdims=True)
    return zc * lax.rsqrt(var + LN_EPS) * g + b


def _ln_kernel(x_ref, g_ref, b_ref, o32_ref, o16_ref):
    y = _ln_rows(x_ref[...], g_ref[...], b_ref[...])
    o32_ref[...] = y
    o16_ref[...] = y.astype(BF16)


def _layer_norm(x, g, b):
    m, d = x.shape
    tm = _pick(m, (512, 256, 128))
    row = pl.BlockSpec((tm, d), lambda i: (i, 0))
    vec = pl.BlockSpec((1, d), lambda i: (0, 0))
    outs = (jax.ShapeDtypeStruct((m, d), F32), jax.ShapeDtypeStruct((m, d), BF16))
    return pl.pallas_call(
        _ln_kernel, grid=(m // tm,), in_specs=[row, vec, vec], out_specs=(row, row),
        out_shape=outs, compiler_params=_cparams(("parallel",)), name="layer_norm",
    )(x, g.reshape(1, d), b.reshape(1, d))


def _mm_kernel(a_ref, w_ref, o_ref):
    o_ref[...] = jnp.dot(a_ref[...], w_ref[...], preferred_element_type=F32).astype(o_ref.dtype)


def _matmul(a, w, out_dtype):
    m, k = a.shape
    n = w.shape[1]
    tm = _pick(m, (1536, 1024, 768, 512, 256))
    tn = _pick(n, (1408, 1024, 512, 256, 128))
    return pl.pallas_call(
        _mm_kernel, grid=(m // tm, n // tn),
        in_specs=[pl.BlockSpec((tm, k), lambda i, j: (i, 0)), pl.BlockSpec((k, tn), lambda i, j: (0, j))],
        out_specs=pl.BlockSpec((tm, tn), lambda i, j: (i, j)),
        out_shape=jax.ShapeDtypeStruct((m, n), out_dtype),
        compiler_params=_cparams(("parallel", "parallel")), name="matmul",
    )(a, w)


def _rope_perm():
    p = np.zeros((LANES, LANES), np.float32)
    half = ROT_DIM // 2
    for c in range(LANES):
        cc = c % HEAD_DIM
        if cc < half:
            p[c + half, c] = -1.0
        elif cc < ROT_DIM:
            p[c - half, c] = 1.0
    return jnp.asarray(p, BF16)


def _rope_tables(tp):
    pos = jnp.arange(tp, dtype=F32)
    inv_freq = ROPE_THETA ** (-jnp.arange(0, ROT_DIM, 2, dtype=F32) / ROT_DIM)
    ang = pos[:, None] * inv_freq[None, :]
    cos, sin = jnp.cos(ang), jnp.sin(ang)
    half = ROT_DIM // 2
    lane = np.arange(LANES) % HEAD_DIM
    rot = jnp.asarray(lane < ROT_DIM)
    idx = jnp.asarray(lane % half)
    cos_t = jnp.where(rot[None, :], cos[:, idx], 1.0)
    sin_t = jnp.where(rot[None, :], sin[:, idx], 0.0)
    return cos_t, sin_t


def _split3(x):
    hi = x.astype(BF16)
    r1 = x - hi.astype(F32)
    mid = r1.astype(BF16)
    lo = (r1 - mid.astype(F32)).astype(BF16)
    return hi, mid, lo


def _placement_constants(n_fox_heads):
    nh = n_fox_heads
    sq = np.zeros((nh * HEAD_DIM, nh * LANES), np.float32)
    for h in range(nh):
        for d in range(HEAD_DIM):
            sq[h * HEAD_DIM + d, h * LANES + d] = 1.0
    eq = np.zeros((3 * LANES, nh * LANES), np.float32)
    ek = np.zeros((3 * LANES, nh * LANES), np.float32)
    cq = np.zeros((1, nh * LANES), np.float32)
    ck = np.zeros((1, nh * LANES), np.float32)
    for h in range(nh):
        for j in range(3):
            eq[j * LANES + IDX_DIM + h, h * LANES + HEAD_DIM + j] = 1.0
            ek[j * LANES + IDX_DIM + h, h * LANES + HEAD_DIM + 3 + j] = -1.0
            cq[0, h * LANES + HEAD_DIM + 3 + j] = 1.0
            ck[0, h * LANES + HEAD_DIM + j] = 1.0
    dup = np.zeros((LANES, LANES), np.float32)
    for d in range(IDX_DIM):
        dup[d, d] = 1.0
        dup[d, d + IDX_DIM] = 1.0
    bf = lambda a: jnp.asarray(a, BF16)
    return bf(sq * HEAD_DIM ** -0.5), bf(sq), bf(eq), bf(ek), jnp.asarray(cq), jnp.asarray(ck), bf(dup)


def _prep_kernel(fq_ref, fk_ref, fv_ref, dq_ref, iq_ref, ckv_ref, misc_ref, cos_ref, sin_ref, perm_ref,
                 kvg_ref, wkv_ref, fb_ref, sq_ref, sk_ref, eq_ref, ek_ref, cq_ref, ck_ref, dup_ref,
                 qa_o, ka_o, fvt_o, dqp_o, dkp_o, dvt_o, iqp_o, ik2_o, iwt_o, carry_ref):
    t = pl.program_id(1)
    cos = cos_ref[...]
    sin = sin_ref[...]
    perm = perm_ref[...]
    tb = cos.shape[0]
    lane = lax.broadcasted_iota(I32, (tb, LANES), 1)
    lower = lane < HEAD_DIM

    def rope_tiles(x):
        outs = []
        for j in range(x.shape[1] // LANES):
            xt = x[:, j * LANES:(j + 1) * LANES]
            yt = jnp.dot(xt.astype(BF16), perm, preferred_element_type=F32)
            outs.append(xt * cos + yt * sin)
        return outs

    def pair_split(tile, scale, out_ref, p):
        v = (tile * scale).astype(BF16)
        zero = jnp.zeros_like(v)
        out_ref[2 * p] = jnp.where(lower, v, zero)
        out_ref[2 * p + 1] = jnp.where(lower, zero, v)

    def values_t(v, out_ref):
        sub = lax.broadcasted_iota(I32, (AUG_ROWS - HEAD_DIM, tb), 0)
        tail = jnp.where(sub == 0, 1.0, 0.0).astype(BF16)
        for p in range(v.shape[1] // LANES):
            vt = v[:, p * LANES:(p + 1) * LANES].astype(F32).T
            for q in range(2):
                out_ref[2 * p + q, 0:HEAD_DIM, :] = vt[q * HEAD_DIM:(q + 1) * HEAD_DIM, :].astype(BF16)
                out_ref[2 * p + q, HEAD_DIM:AUG_ROWS, :] = tail

    for p, tile in enumerate(rope_tiles(dq_ref[...].astype(F32))):
        pair_split(tile, HEAD_DIM ** -0.5, dqp_o, p)
    for p, tile in enumerate(rope_tiles(iq_ref[...].astype(F32))):
        pair_split(tile, IDX_DIM ** -0.5, iqp_o, p)
    ckv = ckv_ref[...].astype(F32)
    ckvn = ckv * lax.rsqrt(jnp.mean(ckv * ckv, axis=-1, keepdims=True) + RMS_EPS) * kvg_ref[...]
    kv = jnp.dot(ckvn.astype(BF16), wkv_ref[...], preferred_element_type=F32)
    half = kv.shape[1] // 2
    for p, tile in enumerate(rope_tiles(kv[:, :half])):
        dkp_o[p] = tile.astype(BF16)
    values_t(kv[:, half:].astype(BF16), dvt_o)
    values_t(fv_ref[...], fvt_o)

    x = misc_ref[...]
    ik = rope_tiles(x)[0].astype(BF16)
    ik2_o[...] = jnp.dot(ik, dup_ref[...], preferred_element_type=F32).astype(BF16)
    nfh = fq_ref.shape[1] // HEAD_DIM
    iwt_o[...] = (x * (IDX_HEADS ** -0.5)).T[IDX_DIM + nfh:IDX_DIM + nfh + IDX_HEADS, :]
    logf = jax.nn.log_sigmoid(x + fb_ref[...])
    logf = jnp.where((lane >= IDX_DIM) & (lane < IDX_DIM + nfh), logf, 0.0)

    @pl.when(t == 0)
    def _():
        carry_ref[...] = jnp.zeros_like(carry_ref)

    r = lax.broadcasted_iota(I32, (tb, tb), 0)
    c = lax.broadcasted_iota(I32, (tb, tb), 1)
    tril = jnp.where(r >= c, 1.0, 0.0).astype(BF16)
    hi, mid, lo = _split3(logf)
    cum = (jnp.dot(tril, hi, preferred_element_type=F32) + jnp.dot(tril, mid, preferred_element_type=F32)
           + jnp.dot(tril, lo, preferred_element_type=F32)) + carry_ref[0:1, :]
    carry_ref[0:1, :] = cum[tb - 1:tb, :]

    cum3 = jnp.concatenate(_split3(cum), axis=1)
    qa = (jnp.dot(fq_ref[...], sq_ref[...], preferred_element_type=F32)
          + jnp.dot(cum3, eq_ref[...], preferred_element_type=F32) + cq_ref[...]).astype(BF16)
    ka = (jnp.dot(fk_ref[...], sk_ref[...], preferred_element_type=F32)
          + jnp.dot(cum3, ek_ref[...], preferred_element_type=F32) + ck_ref[...]).astype(BF16)
    for h in range(nfh):
        qa_o[h] = qa[:, h * LANES:(h + 1) * LANES]
        ka_o[h] = ka[:, h * LANES:(h + 1) * LANES]


def _prep(proj_a, proj_b, cos_t, sin_t, kv_norm_g, w_kv_up, f_bias, bsz, tp):
    tb = SEQ_BLOCK
    na = proj_a.shape[-1]
    pa = proj_a.reshape(bsz, tp, na)
    pb = proj_b.reshape(bsz, tp, LANES)
    fb = jnp.zeros((1, LANES), F32).at[0, IDX_DIM:IDX_DIM + 8].set(f_bias)
    nfh = f_bias.shape[0]
    ndh = 512 // HEAD_DIM
    consts = _placement_constants(nfh)
    bs = lambda w, cb: pl.BlockSpec((None, tb, w), lambda b, t: (b, t, cb))
    tab = pl.BlockSpec((tb, LANES), lambda b, t: (t, 0))
    const = lambda a: pl.BlockSpec(a.shape, lambda b, t: (0,) * a.ndim)
    heads = lambda n: pl.BlockSpec((None, n, tb, LANES), lambda b, t: (b, 0, t, 0))
    heads_t = lambda n: pl.BlockSpec((None, n, AUG_ROWS, tb), lambda b, t: (b, 0, 0, t))
    sds = jax.ShapeDtypeStruct
    out_shapes = (
        sds((bsz, nfh, tp, LANES), BF16), sds((bsz, nfh, tp, LANES), BF16), sds((bsz, nfh, AUG_ROWS, tp), BF16),
        sds((bsz, ndh, tp, LANES), BF16), sds((bsz, ndh // 2, tp, LANES), BF16), sds((bsz, ndh, AUG_ROWS, tp), BF16),
        sds((bsz, IDX_HEADS, tp, LANES), BF16), sds((bsz, tp, LANES), BF16), sds((bsz, IDX_HEADS, tp), F32),
    )
    out_specs = (heads(nfh), heads(nfh), heads_t(nfh), heads(ndh), heads(ndh // 2), heads_t(ndh),
                 heads(IDX_HEADS), bs(LANES, 0), pl.BlockSpec((None, IDX_HEADS, tb), lambda b, t: (b, 0, t)))
    fixed = (_rope_perm(), kv_norm_g.reshape(1, LANES), w_kv_up.astype(BF16), fb) + consts
    return pl.pallas_call(
        _prep_kernel, grid=(bsz, tp // tb),
        in_specs=[bs(512, 2), bs(512, 3), bs(512, 4), bs(512, 5), bs(1024, 3), bs(LANES, 32), bs(LANES, 0),
                  tab, tab] + [const(a) for a in fixed],
        out_specs=out_specs, out_shape=out_shapes, scratch_shapes=[pltpu.VMEM((SUBLANES, LANES), F32)],
        compiler_params=_cparams(("arbitrary", "arbitrary")), name="prep",
    )(pa, pa, pa, pa, pa, pa, pb, cos_t, sin_t, *fixed)


_NT = (((1,), (1,)), ((), ()))


def _attend(q_ref, k_at, v_at, bias_at, n_full, diag_bias, m_ref, acc_ref):
    nh = q_ref.shape[0]
    m_ref[...] = jnp.full(m_ref.shape, NEG_INF, F32)
    acc_ref[...] = jnp.zeros(acc_ref.shape, F32)

    tq = q_ref.shape[1]

    def step(kb, tk, nsub, masked):
        rows = [pl.ds(pl.multiple_of((kb * nsub + j) * tk, tk), tk) for j in range(nsub)]
        bias = [bias_at(r) for r in rows]
        if masked and diag_bias is not None:
            bias = [diag_bias if b is None else b + diag_bias for b in bias]
        units = [(j, h) for j in range(nsub) for h in range(nh)]
        logits = lambda u: lax.dot_general(k_at(u[1], rows[u[0]]), q_ref[u[1]], _NT,
                                           preferred_element_type=F32)
        ahead = 4
        pending = {n: logits(units[n]) for n in range(min(ahead, len(units)))}
        for n, (j, h) in enumerate(units):
            s = pending.pop(n)
            if n + ahead < len(units):
                pending[n + ahead] = logits(units[n + ahead])
            if bias[j] is not None:
                s = bias[j] + s
            m_old = m_ref[h, 0:1, :]
            m_new = jnp.maximum(m_old, jnp.max(s, axis=0, keepdims=True))
            p = jnp.exp(s - m_new)
            acc_ref[h] = (acc_ref[h] * jnp.exp(m_old - m_new)
                          + jnp.dot(v_at(h, rows[j]), p.astype(BF16), preferred_element_type=F32))
            m_ref[h, 0:1, :] = m_new

    def loop_body(tk, nsub, masked, kb, carry):
        step(kb, tk, nsub, masked)
        return carry

    n4 = lax.shift_right_logical(n_full, 2)
    n2 = lax.shift_right_logical(n_full, 1)
    lax.fori_loop(0, n4, functools.partial(loop_body, 2 * tq, 2, False), 0)
    lax.fori_loop(2 * n4, n2, functools.partial(loop_body, 2 * tq, 1, False), 0)
    lax.fori_loop(2 * n2, n_full, functools.partial(loop_body, tq, 1, False), 0)
    lax.fori_loop(n_full, n_full + 1, functools.partial(loop_body, tq, 1, True), 0)
    return jnp.concatenate([acc_ref[h, :HEAD_DIM, :] / acc_ref[h, HEAD_DIM:HEAD_DIM + 1, :] for h in range(nh)],
                           axis=0)


def _diag_bias(tk, tq):
    r = lax.broadcasted_iota(I32, (tk, tq), 0)
    c = lax.broadcasted_iota(I32, (tk, tq), 1)
    return jnp.where(r <= c, 0.0, NEG_INF)


def _group_norm_t(y_t, gain_t):
    ms = jnp.mean(y_t * y_t, axis=0, keepdims=True)
    return (y_t * lax.rsqrt(ms + RMS_EPS) * gain_t).T.astype(BF16)


def _fox_kernel(q_ref, k_ref, v_ref, g_ref, o_ref, m_ref, acc_ref):
    i = pl.program_id(1)
    tq = q_ref.shape[1]
    y_t = _attend(q_ref, lambda h, rows: k_ref[h, rows, :], lambda h, rows: v_ref[h, :, rows],
                  lambda rows: None, i, _diag_bias(tq, tq), m_ref, acc_ref)
    o_ref[...] = _group_norm_t(y_t, g_ref[...])


def _fox_attention(q_aug, k_aug, v_t, gain_t, bsz, tp):
    tq = SEQ_BLOCK
    nh = q_aug.shape[1]
    return pl.pallas_call(
        _fox_kernel, grid=(bsz, tp // tq),
        in_specs=[pl.BlockSpec((None, nh, tq, LANES), lambda b, i: (b, 0, i, 0)),
                  _resident((None, nh, tp, LANES), lambda b, i: (b, 0, 0, 0)),
                  _resident((None, nh, AUG_ROWS, tp), lambda b, i: (b, 0, 0, 0)),
                  _resident((nh * HEAD_DIM, tq), lambda b, i: (0, 0))],
        out_specs=pl.BlockSpec((None, tq, nh * HEAD_DIM), lambda b, i: (b, i, 0)),
        out_shape=jax.ShapeDtypeStruct((bsz, tp, nh * HEAD_DIM), BF16),
        scratch_shapes=[pltpu.VMEM((nh, SUBLANES, tq), F32), pltpu.VMEM((nh, AUG_ROWS, tq), F32)],
        compiler_params=_cparams(("arbitrary", "arbitrary")), name="fox_attention",
    )(q_aug, k_aug, v_t, gain_t)


def _order_key(x):
    b = lax.bitcast_convert_type(x + 0.0, I32)
    return b ^ ((b >> 31) & 0x7FFFFFFF)


def _dsa_kernel(topk, iq_ref, iw_ref, q_ref, ik_ref, k_ref, v_ref, g_ref, o_ref, key_ref, hi_ref, lo_ref,
                m_ref, acc_ref):
    i = pl.program_id(1)
    tq = q_ref.shape[1]
    tk = tq
    diag_bias = _diag_bias(tk, tq)
    blk = lambda kb: pl.ds(pl.multiple_of(kb * tk, tk), tk)
    heads_per_dot = 4

    def score_rows(kb, rows_n, masked):
        rows = pl.ds(pl.multiple_of(kb * rows_n, rows_n), rows_n)
        ik = ik_ref[rows, :]
        sc = jnp.zeros((rows_n, tq), F32)
        for j0 in range(0, IDX_HEADS, heads_per_dot):
            iq = iq_ref[j0:j0 + heads_per_dot].reshape(heads_per_dot * tq, LANES)
            lg = lax.dot_general(ik, iq, _NT, preferred_element_type=F32)
            for j in range(heads_per_dot):
                sc = sc + jnp.maximum(lg[:, j * tq:(j + 1) * tq], 0.0) * iw_ref[j0 + j:j0 + j + 1, :]
        if masked:
            sc = jnp.where(diag_bias == 0.0, sc, NEG_INF)
        key = _order_key(sc)
        key_ref[rows, :] = key
        hi_ref[rows, :] = (key >> 16).astype(I16)
        lo_ref[rows, :] = ((key & 0xFFFF) - HALF_BIAS).astype(I16)

    def score_loop(rows_n, masked, kb, carry):
        score_rows(kb, rows_n, masked)
        return carry

    n_pair = lax.shift_right_logical(i, 1)
    lax.fori_loop(0, n_pair, functools.partial(score_loop, 2 * tk, False), 0)
    lax.fori_loop(2 * n_pair, i, functools.partial(score_loop, tk, False), 0)
    lax.fori_loop(i, i + 1, functools.partial(score_loop, tk, True), 0)
    filler = jnp.full((tk, tq), -HALF_BIAS, I16)
    hi_ref[blk(i + 1), :] = filler
    lo_ref[blk(i + 1), :] = filler

    pack = 2 * SUBLANES
    wide = 2 * tk
    n_wide = lax.shift_right_logical(i + 2, 1)
    wblk = lambda kb: pl.ds(pl.multiple_of(kb * wide, wide), wide)

    lanes_of_count = 2

    def count(ref, pred):
        def body(kb, cnts):
            hit = jnp.where(pred(ref[wblk(kb), :]), jnp.ones((), I16), jnp.zeros((), I16))
            cnts = list(cnts)
            for g in range(wide // pack):
                cnts[g % lanes_of_count] = cnts[g % lanes_of_count] + hit[g * pack:(g + 1) * pack, :]
            return tuple(cnts)
        cnts = lax.fori_loop(0, n_wide, body, (jnp.zeros((pack, tq), I16),) * lanes_of_count)
        total = sum(c.astype(I32) for c in cnts)
        return jnp.sum(total, axis=0, keepdims=True)

    def kth_largest_half(ref, rank):
        t_u = jnp.zeros((1, tq), I32)
        for bit in range(15, -1, -1):
            cand_u = t_u | (1 << bit)
            cand = (cand_u - HALF_BIAS).astype(I16)
            t_u = jnp.where(count(ref, lambda x: x >= cand) >= rank, cand_u, t_u)
        return t_u - HALF_BIAS

    top_hi = kth_largest_half(hi_ref, topk)
    top_hi16 = top_hi.astype(I16)
    above = count(hi_ref, lambda x: x > top_hi16)

    def keep_lo(kb, carry):
        lo_ref[wblk(kb), :] = jnp.where(hi_ref[wblk(kb), :] == top_hi16, lo_ref[wblk(kb), :],
                                        jnp.asarray(-HALF_BIAS, I16))
        return carry

    lax.fori_loop(0, n_wide, keep_lo, 0)
    top_lo = kth_largest_half(lo_ref, topk - above)
    top_lo16 = top_lo.astype(I16)
    at_or_above = above + count(lo_ref, lambda x: x >= top_lo16)
    thr_raw = top_hi * 65536 + (top_lo + HALF_BIAS)
    floor_key = int(np.array(NEG_INF, np.float32).view(np.int32)) ^ 0x7FFFFFFF
    thr = jnp.maximum(thr_raw, np.int32(floor_key + 1))

    surplus = jnp.where(thr_raw == thr, at_or_above - topk, 0)

    @pl.when(jnp.max(surplus) > 0)
    def _():
        def count_rows(pred):
            def body(kb, cnt):
                row = kb * tk + lax.broadcasted_iota(I32, (tk, tq), 0)
                hit = jnp.where(pred(key_ref[blk(kb), :], row), 1, 0).reshape(tk // SUBLANES, SUBLANES, tq)
                return cnt + jnp.sum(hit, axis=0)
            cnt = lax.fori_loop(0, i + 1, body, jnp.zeros((SUBLANES, tq), I32))
            return jnp.sum(cnt, axis=0, keepdims=True)

        keep = topk - count_rows(lambda k, row: k > thr)
        last = jnp.zeros((1, tq), I32)
        for bit in range(int(key_ref.shape[0]).bit_length() - 1, -1, -1):
            cand = last | (1 << bit)
            before = count_rows(lambda k, row: (k == thr) & (row < cand))
            last = jnp.where(before < keep, cand, last)

        def demote(kb, carry):
            row = kb * tk + lax.broadcasted_iota(I32, (tk, tq), 0)
            k = key_ref[blk(kb), :]
            key_ref[blk(kb), :] = jnp.where((k == thr) & (row > last), thr - 1, k)
            return carry

        lax.fori_loop(0, i + 1, demote, 0)

    y_t = _attend(q_ref, lambda h, rows: k_ref[h // 2, rows, :], lambda h, rows: v_ref[h, :, rows],
                  lambda rows: jnp.where(key_ref[rows, :] >= thr, 0.0, NEG_INF), i, None, m_ref, acc_ref)
    o_ref[...] = _group_norm_t(y_t, g_ref[...])


def _dsa_attention(iq_hm, iw_t, q_pad, ik, k_pair, v_t, gain_t, topk, bsz, tp):
    tq = SEQ_BLOCK
    nh = q_pad.shape[1]
    return pl.pallas_call(
        functools.partial(_dsa_kernel, topk), grid=(bsz, tp // tq),
        in_specs=[pl.BlockSpec((None, IDX_HEADS, tq, LANES), lambda b, i: (b, 0, i, 0)),
                  pl.BlockSpec((None, IDX_HEADS, tq), lambda b, i: (b, 0, i)),
                  pl.BlockSpec((None, nh, tq, LANES), lambda b, i: (b, 0, i, 0)),
                  _resident((None, tp, LANES), lambda b, i: (b, 0, 0)),
                  _resident((None, nh // 2, tp, LANES), lambda b, i: (b, 0, 0, 0)),
                  _resident((None, nh, AUG_ROWS, tp), lambda b, i: (b, 0, 0, 0)),
                  _resident((nh * HEAD_DIM, tq), lambda b, i: (0, 0))],
        out_specs=pl.BlockSpec((None, tq, nh * HEAD_DIM), lambda b, i: (b, i, 0)),
        out_shape=jax.ShapeDtypeStruct((bsz, tp, nh * HEAD_DIM), BF16),
        scratch_shapes=[pltpu.VMEM((tp, tq), I32), pltpu.VMEM((tp + tq, tq), I16), pltpu.VMEM((tp + tq, tq), I16),
                        pltpu.VMEM((nh, SUBLANES, tq), F32), pltpu.VMEM((nh, AUG_ROWS, tq), F32)],
        compiler_params=_cparams(("arbitrary", "arbitrary")), name="dsa_attention",
    )(iq_hm, iw_t, q_pad, ik, k_pair, v_t, gain_t)


def _s5_kernel(u_ref, bre_ref, bim_ref, are_ref, aim_ref, cre_ref, cim_ref, d_ref, gw_ref, gb_ref, gn_ref,
               o_ref, sre_ref, sim_ref, cre_carry, cim_carry):
    t = pl.program_id(1)
    tb = u_ref.shape[0]
    ntile = bre_ref.shape[0]
    sw = bre_ref.shape[2]

    @pl.when(t == 0)
    def _():
        cre_carry[...] = jnp.zeros_like(cre_carry)
        cim_carry[...] = jnp.zeros_like(cim_carry)

    u = u_ref[...]
    for l in range(ntile):
        ul = u[:, l * LANES:(l + 1) * LANES]
        sre_ref[:, l * sw:(l + 1) * sw] = jnp.dot(ul, bre_ref[l], preferred_element_type=F32)
        sim_ref[:, l * sw:(l + 1) * sw] = jnp.dot(ul, bim_ref[l], preferred_element_type=F32)

    ar = are_ref[...]
    ai = aim_ref[...]

    def body(r, carry):
        sr, si = carry
        row = pl.ds(r, 1)
        nr = ar * sr - ai * si + sre_ref[row, :]
        ni = ar * si + ai * sr + sim_ref[row, :]
        sre_ref[row, :] = nr
        sim_ref[row, :] = ni
        return nr, ni

    sr, si = lax.fori_loop(0, tb, body, (cre_carry[0:1, :], cim_carry[0:1, :]), unroll=SCAN_UNROLL)
    cre_carry[0:1, :] = sr
    cim_carry[0:1, :] = si

    ys = []
    for l in range(ntile):
        s_re = sre_ref[:, l * sw:(l + 1) * sw].astype(BF16)
        s_im = sim_ref[:, l * sw:(l + 1) * sw].astype(BF16)
        ys.append(jnp.dot(s_re, cre_ref[l], preferred_element_type=F32)
                  - jnp.dot(s_im, cim_ref[l], preferred_element_type=F32))
    y = jnp.concatenate(ys, axis=1) + d_ref[...] * u.astype(F32)
    g = jax.nn.gelu(y)
    gate = jax.nn.sigmoid(jnp.dot(g.astype(BF16), gw_ref[...], preferred_element_type=F32) + gb_ref[...])
    out = g * gate
    ms = jnp.mean(out * out, axis=-1, keepdims=True)
    o_ref[...] = (out * lax.rsqrt(ms + RMS_EPS) * gn_ref[...]).astype(BF16)


def _s5_params(lam_re, lam_im, log_dt, b_re, b_im, c_re, c_im):
    g, p = lam_re.shape
    gpt = LANES // SSM_GROUP
    nt = g // gpt
    dt = jnp.exp(log_dt)[:, None]
    mag = jnp.exp(lam_re * dt)
    ab_re, ab_im = mag * jnp.cos(lam_im * dt), mag * jnp.sin(lam_im * dt)
    den = lam_re * lam_re + lam_im * lam_im
    nr, ni = ab_re - 1.0, ab_im
    zr = (nr * lam_re + ni * lam_im) / den
    zi = (ni * lam_re - nr * lam_im) / den
    bb_re = zr[..., None] * b_re - zi[..., None] * b_im
    bb_im = zr[..., None] * b_im + zi[..., None] * b_re
    eye = jnp.eye(gpt, dtype=F32)

    def in_map(bb):
        bb = bb.reshape(nt, gpt, p, SSM_GROUP)
        m = jnp.einsum("lgpc,gh->lgchp", bb, eye)
        return m.reshape(nt, LANES, gpt * p).astype(BF16)

    def out_map(cc):
        cc = cc.reshape(nt, gpt, SSM_GROUP, p)
        m = jnp.einsum("lgcp,gh->lhpgc", cc, eye)
        return m.reshape(nt, gpt * p, LANES).astype(BF16)

    return (in_map(bb_re), in_map(bb_im), ab_re.reshape(1, g * p), ab_im.reshape(1, g * p),
            out_map(c_re), out_map(c_im))


def _s5_mixer(proj_a, params, d_skip, glu_w, glu_b, gain, bsz, tp):
    tb = SEQ_BLOCK
    b_re, b_im, a_re, a_im, c_re, c_im = params
    nt, _, sw = b_re.shape
    width = nt * LANES
    pa = proj_a.reshape(bsz, tp, proj_a.shape[-1])
    const = lambda shp: _resident(shp, lambda b, t: (0,) * len(shp))
    return pl.pallas_call(
        _s5_kernel, grid=(bsz, tp // tb),
        in_specs=[pl.BlockSpec((None, tb, width), lambda b, t: (b, t, 0)),
                  const(b_re.shape), const(b_im.shape), const(a_re.shape), const(a_im.shape),
                  const(c_re.shape), const(c_im.shape), const((1, width)), const((width, width)),
                  const((1, width)), const((1, width))],
        out_specs=pl.BlockSpec((None, tb, width), lambda b, t: (b, t, 0)),
        out_shape=jax.ShapeDtypeStruct((bsz, tp, width), BF16),
        scratch_shapes=[pltpu.VMEM((tb, nt * sw), F32), pltpu.VMEM((tb, nt * sw), F32),
                        pltpu.VMEM((SUBLANES, nt * sw), F32), pltpu.VMEM((SUBLANES, nt * sw), F32)],
        compiler_params=_cparams(("arbitrary", "arbitrary")), name="s5_mixer",
    )(pa, b_re, b_im, a_re, a_im, c_re, c_im, d_skip.reshape(1, width), glu_w.astype(BF16),
      glu_b.reshape(1, width), gain.reshape(1, width))


def _out_proj_kernel(alpha, ys_ref, yf_ref, yd_ref, w_ref, h_ref, g_ref, b_ref, o32_ref, o16_ref):
    n1 = ys_ref.shape[1]
    n2 = n1 + yf_ref.shape[1]
    m = (jnp.dot(ys_ref[...], w_ref[0:n1, :], preferred_element_type=F32)
         + jnp.dot(yf_ref[...], w_ref[n1:n2, :], preferred_element_type=F32)
         + jnp.dot(yd_ref[...], w_ref[n2:, :], preferred_element_type=F32))
    y = _ln_rows(alpha * h_ref[...] + m, g_ref[...], b_ref[...])
    o32_ref[...] = y
    o16_ref[...] = y.astype(BF16)


def _out_proj(y_ssm, y_fox, y_dsa, w_out, h, g, b, alpha):
    m, d = h.shape
    tm = _pick(m, (512, 256, 128))
    row = lambda w: pl.BlockSpec((tm, w), lambda i: (i, 0))
    vec = pl.BlockSpec((1, d), lambda i: (0, 0))
    return pl.pallas_call(
        functools.partial(_out_proj_kernel, alpha), grid=(m // tm,),
        in_specs=[row(y_ssm.shape[1]), row(y_fox.shape[1]), row(y_dsa.shape[1]),
                  _resident((d, d), lambda i: (0, 0)), row(d), vec, vec],
        out_specs=(row(d), row(d)),
        out_shape=(jax.ShapeDtypeStruct((m, d), F32), jax.ShapeDtypeStruct((m, d), BF16)),
        compiler_params=_cparams(("parallel",)), name="out_proj",
    )(y_ssm, y_fox, y_dsa, w_out.astype(BF16), h, g.reshape(1, d), b.reshape(1, d))


HALO = 16


def _ffn_up_kernel(h_ref, halo_ref, wv_ref, wg_ref, cwv_ref, cwg_ref, cbv_ref, cbg_ref, o_ref):
    ti = pl.program_id(1)
    h = h_ref[...]
    halo = halo_ref[...]
    first = ti == 0

    tf = wv_ref.shape[1]
    parts = 2 if tf % (2 * LANES) == 0 else 1
    pw = tf // parts

    def project(w_ref, cols):
        a = jnp.dot(h, w_ref[:, cols], preferred_element_type=F32)
        ah = jnp.dot(halo, w_ref[:, cols], preferred_element_type=F32)
        return a, ah

    def conv(proj, cw_ref, cb_ref, cols):
        a, ah = proj
        row = lax.broadcasted_iota(I32, a.shape, 0)
        ah = jnp.where(first, 0.0, ah)
        p1 = jnp.where(row == 0, ah[HALO - 1:HALO, :], pltpu.roll(a, 1, axis=0))
        p2 = jnp.where(row == 0, ah[HALO - 2:HALO - 1, :],
                       jnp.where(row == 1, ah[HALO - 1:HALO, :], pltpu.roll(a, 2, axis=0)))
        cw = cw_ref[:, cols]
        return cw[0:1, :] * p2 + cw[1:2, :] * p1 + cw[2:3, :] * a + cb_ref[:, cols]

    col = [slice(p * pw, (p + 1) * pw) for p in range(parts)]
    proj = [(project(wv_ref, c), project(wg_ref, c)) for c in col]
    for (pv, pg), c in zip(proj, col):
        val = conv(pv, cwv_ref, cbv_ref, c)
        gate = conv(pg, cwg_ref, cbg_ref, c)
        o_ref[:, c] = (jax.nn.silu(gate) * val).astype(BF16)


def _ffn_down_kernel(alpha, a_ref, w_ref, h_ref, g_ref, b_ref, o32_ref, o16_ref):
    f = jnp.dot(a_ref[...], w_ref[...], preferred_element_type=F32)
    y = _ln_rows(alpha * h_ref[...] + f, g_ref[...], b_ref[...])
    o32_ref[...] = y
    o16_ref[...] = y.astype(BF16)


def _conv_ffn(h16, h32, w_up, conv_w, conv_b, w_down, g, b, alpha, bsz, tp):
    d = h16.shape[-1]
    dff = w_down.shape[0]
    tm = _pick(tp, (1056, 768, 512, 256))
    tf = _pick(dff, (512, 256, 128))
    nf = dff // tf
    hb = h16.reshape(bsz, tp, d)
    w_up16 = w_up.astype(BF16)
    cw8 = jnp.zeros((SUBLANES, 2 * dff), F32).at[:CONV_WIDTH].set(conv_w)
    cb = conv_b.reshape(1, 2 * dff)
    halo_blocks = tm // HALO
    act = pl.pallas_call(
        _ffn_up_kernel, grid=(bsz, tp // tm, nf),
        in_specs=[pl.BlockSpec((None, tm, d), lambda b, i, j: (b, i, 0)),
                  pl.BlockSpec((None, HALO, d), lambda b, i, j: (b, jnp.maximum(i * halo_blocks - 1, 0), 0)),
                  pl.BlockSpec((d, tf), lambda b, i, j: (0, j)),
                  pl.BlockSpec((d, tf), lambda b, i, j: (0, j + nf)),
                  pl.BlockSpec((SUBLANES, tf), lambda b, i, j: (0, j)),
                  pl.BlockSpec((SUBLANES, tf), lambda b, i, j: (0, j + nf)),
                  pl.BlockSpec((1, tf), lambda b, i, j: (0, j)),
                  pl.BlockSpec((1, tf), lambda b, i, j: (0, j + nf))],
        out_specs=pl.BlockSpec((None, tm, tf), lambda b, i, j: (b, i, j)),
        out_shape=jax.ShapeDtypeStruct((bsz, tp, dff), BF16),
        compiler_params=_cparams(("parallel", "parallel", "parallel")), name="ffn_up_conv",
    )(hb, hb, w_up16, w_up16, cw8, cw8, cb, cb)

    m = bsz * tp
    tr = _pick(m, (384, 256, 128))
    row = lambda w: pl.BlockSpec((tr, w), lambda i: (i, 0))
    vec = pl.BlockSpec((1, d), lambda i: (0, 0))
    return pl.pallas_call(
        functools.partial(_ffn_down_kernel, alpha), grid=(m // tr,),
        in_specs=[row(dff), _resident((dff, d), lambda i: (0, 0)), row(d), vec, vec],
        out_specs=(row(d), row(d)),
        out_shape=(jax.ShapeDtypeStruct((m, d), F32), jax.ShapeDtypeStruct((m, d), BF16)),
        compiler_params=_cparams(("parallel",)), name="ffn_down_norm",
    )(act.reshape(m, dff), w_down.astype(BF16), h32, g.reshape(1, d), b.reshape(1, d))


def _split_w_in(w_in, d_model):
    ssm_w = d_model // 2
    fox_w = d_model // 4
    dsa_w = d_model - ssm_w - fox_w
    kv_rank = d_model // 16
    fox_h = fox_w // HEAD_DIM
    sizes = (ssm_w, fox_w, fox_w, fox_w, fox_h, dsa_w, kv_rank, IDX_HEADS * IDX_DIM, IDX_DIM, IDX_HEADS)
    offs = np.cumsum((0,) + sizes)
    seg = lambda k: w_in[:, offs[k]:offs[k + 1]]
    u, fq, fk, fv, ff, dq, ckv, iq, ik, iw = (seg(k) for k in range(10))
    w_a = jnp.concatenate([u, fq, fk, fv, dq, iq, ckv], axis=1).astype(BF16)
    pad = jnp.zeros((w_in.shape[0], LANES - IDX_DIM - fox_h - IDX_HEADS), w_in.dtype)
    w_b = jnp.concatenate([ik, ff, iw, pad], axis=1).astype(BF16)
    return w_a, w_b


def _mixer(h16, h32, lp, cos_t, sin_t, topk, bsz, tp, alpha):
    d = h32.shape[-1]
    w_a, w_b = lp["w_in_split"]
    proj_a = _matmul(h16, w_a, BF16)
    proj_b = _matmul(h16, w_b, F32)
    q_aug, k_aug, fv_t, q_pad, k_pair, dv_t, iq_pad, ik2, iw_t = _prep(
        proj_a, proj_b, cos_t, sin_t, lp["kv_norm_g"], lp["w_kv_up"], lp["fox_f_bias"], bsz, tp)
    ssm_w = d // 2
    fox_w = d // 4
    g_ssm, g_fox, g_dsa = jnp.split(lp["mix_norm_g"], (ssm_w, ssm_w + fox_w))
    gain = lambda g: jnp.broadcast_to(g[:, None], (g.shape[0], SEQ_BLOCK))

    y_ssm = _s5_mixer(proj_a, lp["s5_maps"], lp["ssm_d"], lp["ssm_glu_w"], lp["ssm_glu_b"], g_ssm, bsz, tp)
    y_fox = _fox_attention(q_aug, k_aug, fv_t, gain(g_fox), bsz, tp)
    y_dsa = _dsa_attention(iq_pad, iw_t, q_pad, ik2, k_pair, dv_t, gain(g_dsa), topk, bsz, tp)
    return _out_proj(y_ssm.reshape(bsz * tp, -1), y_fox.reshape(bsz * tp, -1), y_dsa.reshape(bsz * tp, -1),
                     lp["w_out"], h32, lp["ln1_g"], lp["ln1_b"], alpha)


def kernel(x, meta_tokens, ln_in_g, ln_in_b, w_in, ssm_lam_re, ssm_lam_im, ssm_log_dt, ssm_b_re, ssm_b_im,
           ssm_c_re, ssm_c_im, ssm_d, ssm_glu_w, ssm_glu_b, fox_f_bias, kv_norm_g, w_kv_up, mix_norm_g, w_out,
           ln1_g, ln1_b, w_up, conv_w, conv_b, w_down, ln2_g, ln2_b):
    bsz, seq, d = x.shape
    depth = w_in.shape[0]
    alpha = (2.0 * depth) ** 0.25
    topk = min(TOPK_MAX, seq // 4)
    t_real = seq + N_META
    tp = -(-t_real // SEQ_BLOCK) * SEQ_BLOCK

    meta = jnp.broadcast_to(meta_tokens[None].astype(x.dtype), (bsz, N_META, d))
    xcat = jnp.concatenate([meta, x, jnp.zeros((bsz, tp - t_real, d), x.dtype)], axis=1)
    h32, h16 = _layer_norm(xcat.reshape(bsz * tp, d), ln_in_g, ln_in_b)
    cos_t, sin_t = _rope_tables(tp)

    stacked = dict(w_in_split=jax.vmap(lambda w: _split_w_in(w, d))(w_in),
                   s5_maps=jax.vmap(_s5_params)(ssm_lam_re, ssm_lam_im, ssm_log_dt, ssm_b_re, ssm_b_im, ssm_c_re,
                                                ssm_c_im),
                   ssm_d=ssm_d, ssm_glu_w=ssm_glu_w, ssm_glu_b=ssm_glu_b, fox_f_bias=fox_f_bias,
                   kv_norm_g=kv_norm_g, w_kv_up=w_kv_up, mix_norm_g=mix_norm_g, w_out=w_out, ln1_g=ln1_g,
                   ln1_b=ln1_b, w_up=w_up, conv_w=conv_w, conv_b=conv_b, w_down=w_down, ln2_g=ln2_g, ln2_b=ln2_b)
    for l in range(depth):
        lp = jax.tree.map(lambda v: v[l], stacked)
        h32, h16 = _mixer(h16, h32, lp, cos_t, sin_t, topk, bsz, tp, alpha)
        h32, h16 = _conv_ffn(h16, h32, lp["w_up"], lp["conv_w"], lp["conv_b"], lp["w_down"], lp["ln2_g"],
                             lp["ln2_b"], alpha, bsz, tp)
    return h32.reshape(bsz, tp, d)[:, N_META:t_real, :]
```

```python
import functools
import math

import jax
import jax.numpy as jnp
import numpy as np
from jax import lax
from jax.experimental import pallas as pl
from jax.experimental.pallas import tpu as pltpu

F32 = jnp.float32
BF16 = jnp.bfloat16
I32 = jnp.int32
I16 = jnp.int16
HALF_BIAS = 1 << 15

N_META = 16
HEAD_DIM = 64
SSM_GROUP = 16
SSM_STATE = 64
IDX_HEADS = 16
IDX_DIM = 64
TOPK_MAX = 256
ROPE_THETA = 500000.0
ROT_DIM = HEAD_DIM // 4
CONV_WIDTH = 3
LN_EPS = 1e-5
RMS_EPS = 1e-6
NEG_INF = -1e30

LANES = 128
SUBLANES = 8
VMEM_LIMIT_BYTES = 56 * 1024 * 1024

SEQ_BLOCK = 256
AUG_ROWS = 80
INT_MIN = -(2 ** 31)
SCAN_UNROLL = 4


def _pick(n, candidates):
    for c in candidates:
        if n % c == 0:
            return c
    raise ValueError(f"no tile for {n} in {candidates}")


def _cparams(sem, vmem=VMEM_LIMIT_BYTES):
    return pltpu.CompilerParams(dimension_semantics=sem, vmem_limit_bytes=vmem)


def _resident(shape, index_map):
    return pl.BlockSpec(shape, index_map, pipeline_mode=pl.Buffered(1))


def _ln_rows(z, g, b):
    mu = jnp.mean(z, axis=-1, keepdims=True)
    zc = z - mu
    var = jnp.mean(zc * zc, axis=-1, keepdims=True)
    return zc * lax.rsqrt(var + LN_EPS) * g + b


def _ln_kernel(x_ref, g_ref, b_ref, o32_ref, o16_ref):
    y = _ln_rows(x_ref[...], g_ref[...], b_ref[...])
    o32_ref[...] = y
    o16_ref[...] = y.astype(BF16)


def _layer_norm(x, g, b):
    m, d = x.shape
    tm = _pick(m, (512, 256, 128))
    row = pl.BlockSpec((tm, d), lambda i: (i, 0))
    vec = pl.BlockSpec((1, d), lambda i: (0, 0))
    outs = (jax.ShapeDtypeStruct((m, d), F32), jax.ShapeDtypeStruct((m, d), BF16))
    return pl.pallas_call(
        _ln_kernel, grid=(m // tm,), in_specs=[row, vec, vec], out_specs=(row, row),
        out_shape=outs, compiler_params=_cparams(("parallel",)), name="layer_norm",
    )(x, g.reshape(1, d), b.reshape(1, d))


def _mm_kernel(a_ref, w_ref, o_ref):
    o_ref[...] = jnp.dot(a_ref[...], w_ref[...], preferred_element_type=F32).astype(o_ref.dtype)


def _matmul(a, w, out_dtype):
    m, k = a.shape
    n = w.shape[1]
    tm = _pick(m, (1536, 1024, 768, 512, 256))
    tn = _pick(n, (1408, 1024, 512, 256, 128))
    return pl.pallas_call(
        _mm_kernel, grid=(m // tm, n // tn),
        in_specs=[pl.BlockSpec((tm, k), lambda i, j: (i, 0)), pl.BlockSpec((k, tn), lambda i, j: (0, j))],
        out_specs=pl.BlockSpec((tm, tn), lambda i, j: (i, j)),
        out_shape=jax.ShapeDtypeStruct((m, n), out_dtype),
        compiler_params=_cparams(("parallel", "parallel")), name="matmul",
    )(a, w)


def _rope_perm():
    p = np.zeros((LANES, LANES), np.float32)
    half = ROT_DIM // 2
    for c in range(LANES):
        cc = c % HEAD_DIM
        if cc < half:
            p[c + half, c] = -1.0
        elif cc < ROT_DIM:
            p[c - half, c] = 1.0
    return jnp.asarray(p, BF16)


def _rope_tables(tp):
    pos = jnp.arange(tp, dtype=F32)
    inv_freq = ROPE_THETA ** (-jnp.arange(0, ROT_DIM, 2, dtype=F32) / ROT_DIM)
    ang = pos[:, None] * inv_freq[None, :]
    cos, sin = jnp.cos(ang), jnp.sin(ang)
    half = ROT_DIM // 2
    lane = np.arange(LANES) % HEAD_DIM
    rot = jnp.asarray(lane < ROT_DIM)
    idx = jnp.asarray(lane % half)
    cos_t = jnp.where(rot[None, :], cos[:, idx], 1.0)
    sin_t = jnp.where(rot[None, :], sin[:, idx], 0.0)
    return cos_t, sin_t


def _split3(x):
    hi = x.astype(BF16)
    r1 = x - hi.astype(F32)
    mid = r1.astype(BF16)
    lo = (r1 - mid.astype(F32)).astype(BF16)
    return hi, mid, lo


def _placement_constants(n_fox_heads):
    nh = n_fox_heads
    sq = np.zeros((nh * HEAD_DIM, nh * LANES), np.float32)
    for h in range(nh):
        for d in range(HEAD_DIM):
            sq[h * HEAD_DIM + d, h * LANES + d] = 1.0
    eq = np.zeros((3 * LANES, nh * LANES), np.float32)
    ek = np.zeros((3 * LANES, nh * LANES), np.float32)
    cq = np.zeros((1, nh * LANES), np.float32)
    ck = np.zeros((1, nh * LANES), np.float32)
    for h in range(nh):
        for j in range(3):
            eq[j * LANES + IDX_DIM + h, h * LANES + HEAD_DIM + j] = 1.0
            ek[j * LANES + IDX_DIM + h, h * LANES + HEAD_DIM + 3 + j] = -1.0
            cq[0, h * LANES + HEAD_DIM + 3 + j] = 1.0
            ck[0, h * LANES + HEAD_DIM + j] = 1.0
    dup = np.zeros((LANES, LANES), np.float32)
    for d in range(IDX_DIM):
        dup[d, d] = 1.0
        dup[d, d + IDX_DIM] = 1.0
    bf = lambda a: jnp.asarray(a, BF16)
    return bf(sq * HEAD_DIM ** -0.5), bf(sq), bf(eq), bf(ek), jnp.asarray(cq), jnp.asarray(ck), bf(dup)


def _prep_kernel(fq_ref, fk_ref, fv_ref, dq_ref, iq_ref, ckv_ref, misc_ref, cos_ref, sin_ref, perm_ref,
                 kvg_ref, wkv_ref, fb_ref, sq_ref, sk_ref, eq_ref, ek_ref, cq_ref, ck_ref, dup_ref,
                 qa_o, ka_o, fvt_o, dqp_o, dkp_o, dvt_o, iqp_o, ik2_o, iwt_o, carry_ref):
    t = pl.program_id(1)
    cos = cos_ref[...]
    sin = sin_ref[...]
    perm = perm_ref[...]
    tb = cos.shape[0]
    lane = lax.broadcasted_iota(I32, (tb, LANES), 1)
    lower = lane < HEAD_DIM

    def rope_tiles(x):
        outs = []
        for j in range(x.shape[1] // LANES):
            xt = x[:, j * LANES:(j + 1) * LANES]
            yt = jnp.dot(xt.astype(BF16), perm, preferred_element_type=F32)
            outs.append(xt * cos + yt * sin)
        return outs

    def pair_split(tile, scale, out_ref, p):
        v = (tile * scale).astype(BF16)
        zero = jnp.zeros_like(v)
        out_ref[2 * p] = jnp.where(lower, v, zero)
        out_ref[2 * p + 1] = jnp.where(lower, zero, v)

    def values_t(v, out_ref):
        sub = lax.broadcasted_iota(I32, (AUG_ROWS - HEAD_DIM, tb), 0)
        tail = jnp.where(sub == 0, 1.0, 0.0).astype(BF16)
        for p in range(v.shape[1] // LANES):
            vt = v[:, p * LANES:(p + 1) * LANES].astype(F32).T
            for q in range(2):
                out_ref[2 * p + q, 0:HEAD_DIM, :] = vt[q * HEAD_DIM:(q + 1) * HEAD_DIM, :].astype(BF16)
                out_ref[2 * p + q, HEAD_DIM:AUG_ROWS, :] = tail

    for p, tile in enumerate(rope_tiles(dq_ref[...].astype(F32))):
        pair_split(tile, HEAD_DIM ** -0.5, dqp_o, p)
    for p, tile in enumerate(rope_tiles(iq_ref[...].astype(F32))):
        pair_split(tile, IDX_DIM ** -0.5, iqp_o, p)
    ckv = ckv_ref[...].astype(F32)
    ckvn = ckv * lax.rsqrt(jnp.mean(ckv * ckv, axis=-1, keepdims=True) + RMS_EPS) * kvg_ref[...]
    kv = jnp.dot(ckvn.astype(BF16), wkv_ref[...], preferred_element_type=F32)
    half = kv.shape[1] // 2
    for p, tile in enumerate(rope_tiles(kv[:, :half])):
        dkp_o[p] = tile.astype(BF16)
    values_t(kv[:, half:].astype(BF16), dvt_o)
    values_t(fv_ref[...], fvt_o)

    x = misc_ref[...]
    ik = rope_tiles(x)[0].astype(BF16)
    ik2_o[...] = jnp.dot(ik, dup_ref[...], preferred_element_type=F32).astype(BF16)
    nfh = fq_ref.shape[1] // HEAD_DIM
    iwt_o[...] = (x * (IDX_HEADS ** -0.5)).T[IDX_DIM + nfh:IDX_DIM + nfh + IDX_HEADS, :]
    logf = jax.nn.log_sigmoid(x + fb_ref[...])
    logf = jnp.where((lane >= IDX_DIM) & (lane < IDX_DIM + nfh), logf, 0.0)

    @pl.when(t == 0)
    def _():
        carry_ref[...] = jnp.zeros_like(carry_ref)

    r = lax.broadcasted_iota(I32, (tb, tb), 0)
    c = lax.broadcasted_iota(I32, (tb, tb), 1)
    tril = jnp.where(r >= c, 1.0, 0.0).astype(BF16)
    hi, mid, lo = _split3(logf)
    cum = (jnp.dot(tril, hi, preferred_element_type=F32) + jnp.dot(tril, mid, preferred_element_type=F32)
           + jnp.dot(tril, lo, preferred_element_type=F32)) + carry_ref[0:1, :]
    carry_ref[0:1, :] = cum[tb - 1:tb, :]

    cum3 = jnp.concatenate(_split3(cum), axis=1)
    qa = (jnp.dot(fq_ref[...], sq_ref[...], preferred_element_type=F32)
          + jnp.dot(cum3, eq_ref[...], preferred_element_type=F32) + cq_ref[...]).astype(BF16)
    ka = (jnp.dot(fk_ref[...], sk_ref[...], preferred_element_type=F32)
          + jnp.dot(cum3, ek_ref[...], preferred_element_type=F32) + ck_ref[...]).astype(BF16)
    for h in range(nfh):
        qa_o[h] = qa[:, h * LANES:(h + 1) * LANES]
        ka_o[h] = ka[:, h * LANES:(h + 1) * LANES]


def _prep(proj_a, proj_b, cos_t, sin_t, kv_norm_g, w_kv_up, f_bias, bsz, tp):
    tb = SEQ_BLOCK
    na = proj_a.shape[-1]
    pa = proj_a.reshape(bsz, tp, na)
    pb = proj_b.reshape(bsz, tp, LANES)
    fb = jnp.zeros((1, LANES), F32).at[0, IDX_DIM:IDX_DIM + 8].set(f_bias)
    nfh = f_bias.shape[0]
    ndh = 512 // HEAD_DIM
    consts = _placement_constants(nfh)
    bs = lambda w, cb: pl.BlockSpec((None, tb, w), lambda b, t: (b, t, cb))
    tab = pl.BlockSpec((tb, LANES), lambda b, t: (t, 0))
    const = lambda a: pl.BlockSpec(a.shape, lambda b, t: (0,) * a.ndim)
    heads = lambda n: pl.BlockSpec((None, n, tb, LANES), lambda b, t: (b, 0, t, 0))
    heads_t = lambda n: pl.BlockSpec((None, n, AUG_ROWS, tb), lambda b, t: (b, 0, 0, t))
    sds = jax.ShapeDtypeStruct
    out_shapes = (
        sds((bsz, nfh, tp, LANES), BF16), sds((bsz, nfh, tp, LANES), BF16), sds((bsz, nfh, AUG_ROWS, tp), BF16),
        sds((bsz, ndh, tp, LANES), BF16), sds((bsz, ndh // 2, tp, LANES), BF16), sds((bsz, ndh, AUG_ROWS, tp), BF16),
        sds((bsz, IDX_HEADS, tp, LANES), BF16), sds((bsz, tp, LANES), BF16), sds((bsz, IDX_HEADS, tp), F32),
    )
    out_specs = (heads(nfh), heads(nfh), heads_t(nfh), heads(ndh), heads(ndh // 2), heads_t(ndh),
                 heads(IDX_HEADS), bs(LANES, 0), pl.BlockSpec((None, IDX_HEADS, tb), lambda b, t: (b, 0, t)))
    fixed = (_rope_perm(), kv_norm_g.reshape(1, LANES), w_kv_up.astype(BF16), fb) + consts
    return pl.pallas_call(
        _prep_kernel, grid=(bsz, tp // tb),
        in_specs=[bs(512, 2), bs(512, 3), bs(512, 4), bs(512, 5), bs(1024, 3), bs(LANES, 32), bs(LANES, 0),
                  tab, tab] + [const(a) for a in fixed],
        out_specs=out_specs, out_shape=out_shapes, scratch_shapes=[pltpu.VMEM((SUBLANES, LANES), F32)],
        compiler_params=_cparams(("arbitrary", "arbitrary")), name="prep",
    )(pa, pa, pa, pa, pa, pa, pb, cos_t, sin_t, *fixed)


_NT = (((1,), (1,)), ((), ()))


def _attend(q_ref, k_at, v_at, bias_at, n_full, diag_bias, m_ref, acc_ref):
    nh = q_ref.shape[0]
    m_ref[...] = jnp.full(m_ref.shape, NEG_INF, F32)
    acc_ref[...] = jnp.zeros(acc_ref.shape, F32)

    tq = q_ref.shape[1]

    def step(kb, tk, nsub, masked):
        rows = [pl.ds(pl.multiple_of((kb * nsub + j) * tk, tk), tk) for j in range(nsub)]
        bias = [bias_at(r) for r in rows]
        if masked and diag_bias is not None:
            bias = [diag_bias if b is None else b + diag_bias for b in bias]
        units = [(j, h) for j in range(nsub) for h in range(nh)]
        logits = lambda u: lax.dot_general(k_at(u[1], rows[u[0]]), q_ref[u[1]], _NT,
                                           preferred_element_type=F32)
        ahead = 4
        pending = {n: logits(units[n]) for n in range(min(ahead, len(units)))}
        for n, (j, h) in enumerate(units):
            s = pending.pop(n)
            if n + ahead < len(units):
                pending[n + ahead] = logits(units[n + ahead])
            if bias[j] is not None:
                s = bias[j] + s
            m_old = m_ref[h, 0:1, :]
            m_new = jnp.maximum(m_old, jnp.max(s, axis=0, keepdims=True))
            p = jnp.exp(s - m_new)
            acc_ref[h] = (acc_ref[h] * jnp.exp(m_old - m_new)
                          + jnp.dot(v_at(h, rows[j]), p.astype(BF16), preferred_element_type=F32))
            m_ref[h, 0:1, :] = m_new

    def loop_body(tk, nsub, masked, kb, carry):
        step(kb, tk, nsub, masked)
        return carry

    n4 = lax.shift_right_logical(n_full, 2)
    n2 = lax.shift_right_logical(n_full, 1)
    lax.fori_loop(0, n4, functools.partial(loop_body, 2 * tq, 2, False), 0)
    lax.fori_loop(2 * n4, n2, functools.partial(loop_body, 2 * tq, 1, False), 0)
    lax.fori_loop(2 * n2, n_full, functools.partial(loop_body, tq, 1, False), 0)
    lax.fori_loop(n_full, n_full + 1, functools.partial(loop_body, tq, 1, True), 0)
    return jnp.concatenate([acc_ref[h, :HEAD_DIM, :] / acc_ref[h, HEAD_DIM:HEAD_DIM + 1, :] for h in range(nh)],
                           axis=0)


def _diag_bias(tk, tq):
    r = lax.broadcasted_iota(I32, (tk, tq), 0)
    c = lax.broadcasted_iota(I32, (tk, tq), 1)
    return jnp.where(r <= c, 0.0, NEG_INF)


def _group_norm_t(y_t, gain_t):
    ms = jnp.mean(y_t * y_t, axis=0, keepdims=True)
    return (y_t * lax.rsqrt(ms + RMS_EPS) * gain_t).T.astype(BF16)


def _fox_kernel(q_ref, k_ref, v_ref, g_ref, o_ref, m_ref, acc_ref):
    i = pl.program_id(1)
    tq = q_ref.shape[1]
    y_t = _attend(q_ref, lambda h, rows: k_ref[h, rows, :], lambda h, rows: v_ref[h, :, rows],
                  lambda rows: None, i, _diag_bias(tq, tq), m_ref, acc_ref)
    o_ref[...] = _group_norm_t(y_t, g_ref[...])


def _fox_attention(q_aug, k_aug, v_t, gain_t, bsz, tp):
    tq = SEQ_BLOCK
    nh = q_aug.shape[1]
    return pl.pallas_call(
        _fox_kernel, grid=(bsz, tp // tq),
        in_specs=[pl.BlockSpec((None, nh, tq, LANES), lambda b, i: (b, 0, i, 0)),
                  _resident((None, nh, tp, LANES), lambda b, i: (b, 0, 0, 0)),
                  _resident((None, nh, AUG_ROWS, tp), lambda b, i: (b, 0, 0, 0)),
                  _resident((nh * HEAD_DIM, tq), lambda b, i: (0, 0))],
        out_specs=pl.BlockSpec((None, tq, nh * HEAD_DIM), lambda b, i: (b, i, 0)),
        out_shape=jax.ShapeDtypeStruct((bsz, tp, nh * HEAD_DIM), BF16),
        scratch_shapes=[pltpu.VMEM((nh, SUBLANES, tq), F32), pltpu.VMEM((nh, AUG_ROWS, tq), F32)],
        compiler_params=_cparams(("arbitrary", "arbitrary")), name="fox_attention",
    )(q_aug, k_aug, v_t, gain_t)


def _order_key(x):
    b = lax.bitcast_convert_type(x + 0.0, I32)
    return b ^ ((b >> 31) & 0x7FFFFFFF)


def _dsa_kernel(topk, iq_ref, iw_ref, q_ref, ik_ref, k_ref, v_ref, g_ref, o_ref, key_ref, hi_ref, lo_ref,
                m_ref, acc_ref):
    i = pl.program_id(1)
    tq = q_ref.shape[1]
    tk = tq
    diag_bias = _diag_bias(tk, tq)
    blk = lambda kb: pl.ds(pl.multiple_of(kb * tk, tk), tk)
    heads_per_dot = 4

    def score_rows(kb, rows_n, masked):
        rows = pl.ds(pl.multiple_of(kb * rows_n, rows_n), rows_n)
        ik = ik_ref[rows, :]
        sc = jnp.zeros((rows_n, tq), F32)
        for j0 in range(0, IDX_HEADS, heads_per_dot):
            iq = iq_ref[j0:j0 + heads_per_dot].reshape(heads_per_dot * tq, LANES)
            lg = lax.dot_general(ik, iq, _NT, preferred_element_type=F32)
            for j in range(heads_per_dot):
                sc = sc + jnp.maximum(lg[:, j * tq:(j + 1) * tq], 0.0) * iw_ref[j0 + j:j0 + j + 1, :]
        if masked:
            sc = jnp.where(diag_bias == 0.0, sc, NEG_INF)
        key = _order_key(sc)
        key_ref[rows, :] = key
        hi_ref[rows, :] = (key >> 16).astype(I16)
        lo_ref[rows, :] = ((key & 0xFFFF) - HALF_BIAS).astype(I16)

    def score_loop(rows_n, masked, kb, carry):
        score_rows(kb, rows_n, masked)
        return carry

    n_quad = lax.shift_right_logical(i, 2)
    n_pair = lax.shift_right_logical(i, 1)
    lax.fori_loop(0, n_quad, functools.partial(score_loop, 4 * tk, False), 0)
    lax.fori_loop(2 * n_quad, n_pair, functools.partial(score_loop, 2 * tk, False), 0)
    lax.fori_loop(2 * n_pair, i, functools.partial(score_loop, tk, False), 0)
    lax.fori_loop(i, i + 1, functools.partial(score_loop, tk, True), 0)
    filler = jnp.full((tk, tq), -HALF_BIAS, I16)
    hi_ref[blk(i + 1), :] = filler
    lo_ref[blk(i + 1), :] = filler

    pack = 2 * SUBLANES
    wide = 2 * tk
    n_wide = lax.shift_right_logical(i + 2, 1)
    wblk = lambda kb: pl.ds(pl.multiple_of(kb * wide, wide), wide)

    lanes_of_count = 2

    def count(ref, pred):
        def body(kb, cnts):
            hit = jnp.where(pred(ref[wblk(kb), :]), jnp.ones((), I16), jnp.zeros((), I16))
            cnts = list(cnts)
            for g in range(wide // pack):
                cnts[g % lanes_of_count] = cnts[g % lanes_of_count] + hit[g * pack:(g + 1) * pack, :]
            return tuple(cnts)
        cnts = lax.fori_loop(0, n_wide, body, (jnp.zeros((pack, tq), I16),) * lanes_of_count)
        total = sum(c.astype(I32) for c in cnts)
        return jnp.sum(total, axis=0, keepdims=True)

    def kth_largest_half(ref, rank):
        t_u = jnp.zeros((1, tq), I32)
        for bit in range(15, -1, -1):
            cand_u = t_u | (1 << bit)
            cand = (cand_u - HALF_BIAS).astype(I16)
            t_u = jnp.where(count(ref, lambda x: x >= cand) >= rank, cand_u, t_u)
        return t_u - HALF_BIAS

    top_hi = kth_largest_half(hi_ref, topk)
    top_hi16 = top_hi.astype(I16)
    above = count(hi_ref, lambda x: x > top_hi16)

    def keep_lo(kb, carry):
        lo_ref[wblk(kb), :] = jnp.where(hi_ref[wblk(kb), :] == top_hi16, lo_ref[wblk(kb), :],
                                        jnp.asarray(-HALF_BIAS, I16))
        return carry

    lax.fori_loop(0, n_wide, keep_lo, 0)
    top_lo = kth_largest_half(lo_ref, topk - above)
    top_lo16 = top_lo.astype(I16)
    at_or_above = above + count(lo_ref, lambda x: x >= top_lo16)
    thr_raw = top_hi * 65536 + (top_lo + HALF_BIAS)
    floor_key = int(np.array(NEG_INF, np.float32).view(np.int32)) ^ 0x7FFFFFFF
    thr = jnp.maximum(thr_raw, np.int32(floor_key + 1))

    surplus = jnp.where(thr_raw == thr, at_or_above - topk, 0)

    @pl.when(jnp.max(surplus) > 0)
    def _():
        def count_rows(pred):
            def body(kb, cnt):
                row = kb * tk + lax.broadcasted_iota(I32, (tk, tq), 0)
                hit = jnp.where(pred(key_ref[blk(kb), :], row), 1, 0).reshape(tk // SUBLANES, SUBLANES, tq)
                return cnt + jnp.sum(hit, axis=0)
            cnt = lax.fori_loop(0, i + 1, body, jnp.zeros((SUBLANES, tq), I32))
            return jnp.sum(cnt, axis=0, keepdims=True)

        keep = topk - count_rows(lambda k, row: k > thr)
        last = jnp.zeros((1, tq), I32)
        for bit in range(int(key_ref.shape[0]).bit_length() - 1, -1, -1):
            cand = last | (1 << bit)
            before = count_rows(lambda k, row: (k == thr) & (row < cand))
            last = jnp.where(before < keep, cand, last)

        def demote(kb, carry):
            row = kb * tk + lax.broadcasted_iota(I32, (tk, tq), 0)
            k = key_ref[blk(kb), :]
            key_ref[blk(kb), :] = jnp.where((k == thr) & (row > last), thr - 1, k)
            return carry

        lax.fori_loop(0, i + 1, demote, 0)

    y_t = _attend(q_ref, lambda h, rows: k_ref[h // 2, rows, :], lambda h, rows: v_ref[h, :, rows],
                  lambda rows: jnp.where(key_ref[rows, :] >= thr, 0.0, NEG_INF), i, None, m_ref, acc_ref)
    o_ref[...] = _group_norm_t(y_t, g_ref[...])


def _dsa_attention(iq_hm, iw_t, q_pad, ik, k_pair, v_t, gain_t, topk, bsz, tp):
    tq = SEQ_BLOCK
    nh = q_pad.shape[1]
    return pl.pallas_call(
        functools.partial(_dsa_kernel, topk), grid=(bsz, tp // tq),
        in_specs=[pl.BlockSpec((None, IDX_HEADS, tq, LANES), lambda b, i: (b, 0, i, 0)),
                  pl.BlockSpec((None, IDX_HEADS, tq), lambda b, i: (b, 0, i)),
                  pl.BlockSpec((None, nh, tq, LANES), lambda b, i: (b, 0, i, 0)),
                  _resident((None, tp, LANES), lambda b, i: (b, 0, 0)),
                  _resident((None, nh // 2, tp, LANES), lambda b, i: (b, 0, 0, 0)),
                  _resident((None, nh, AUG_ROWS, tp), lambda b, i: (b, 0, 0, 0)),
                  _resident((nh * HEAD_DIM, tq), lambda b, i: (0, 0))],
        out_specs=pl.BlockSpec((None, tq, nh * HEAD_DIM), lambda b, i: (b, i, 0)),
        out_shape=jax.ShapeDtypeStruct((bsz, tp, nh * HEAD_DIM), BF16),
        scratch_shapes=[pltpu.VMEM((tp, tq), I32), pltpu.VMEM((tp + tq, tq), I16), pltpu.VMEM((tp + tq, tq), I16),
                        pltpu.VMEM((nh, SUBLANES, tq), F32), pltpu.VMEM((nh, AUG_ROWS, tq), F32)],
        compiler_params=_cparams(("arbitrary", "arbitrary")), name="dsa_attention",
    )(iq_hm, iw_t, q_pad, ik, k_pair, v_t, gain_t)


def _s5_kernel(u_ref, bre_ref, bim_ref, are_ref, aim_ref, cre_ref, cim_ref, d_ref, gw_ref, gb_ref, gn_ref,
               o_ref, sre_ref, sim_ref, cre_carry, cim_carry):
    t = pl.program_id(0)
    nb, tb = u_ref.shape[0], u_ref.shape[1]
    ntile = bre_ref.shape[0]
    sw = bre_ref.shape[2]

    @pl.when(t == 0)
    def _():
        cre_carry[...] = jnp.zeros_like(cre_carry)
        cim_carry[...] = jnp.zeros_like(cim_carry)

    for b in range(nb):
        u = u_ref[b]
        for l in range(ntile):
            ul = u[:, l * LANES:(l + 1) * LANES]
            sre_ref[b, :, l * sw:(l + 1) * sw] = jnp.dot(ul, bre_ref[l], preferred_element_type=F32)
            sim_ref[b, :, l * sw:(l + 1) * sw] = jnp.dot(ul, bim_ref[l], preferred_element_type=F32)

    ar = are_ref[...]
    ai = aim_ref[...]

    def body(r, carry):
        row = pl.ds(r, 1)
        out = []
        for b, (sr, si) in enumerate(carry):
            nr = ar * sr - ai * si + sre_ref[b, row, :]
            ni = ar * si + ai * sr + sim_ref[b, row, :]
            sre_ref[b, row, :] = nr
            sim_ref[b, row, :] = ni
            out.append((nr, ni))
        return tuple(out)

    init = tuple((cre_carry[b, 0:1, :], cim_carry[b, 0:1, :]) for b in range(nb))
    last = lax.fori_loop(0, tb, body, init, unroll=SCAN_UNROLL)
    for b, (sr, si) in enumerate(last):
        cre_carry[b, 0:1, :] = sr
        cim_carry[b, 0:1, :] = si

    for b in range(nb):
        ys = []
        for l in range(ntile):
            s_re = sre_ref[b, :, l * sw:(l + 1) * sw].astype(BF16)
            s_im = sim_ref[b, :, l * sw:(l + 1) * sw].astype(BF16)
            ys.append(jnp.dot(s_re, cre_ref[l], preferred_element_type=F32)
                      - jnp.dot(s_im, cim_ref[l], preferred_element_type=F32))
        y = jnp.concatenate(ys, axis=1) + d_ref[...] * u_ref[b].astype(F32)
        g = jax.nn.gelu(y)
        gate = jax.nn.sigmoid(jnp.dot(g.astype(BF16), gw_ref[...], preferred_element_type=F32) + gb_ref[...])
        out = g * gate
        ms = jnp.mean(out * out, axis=-1, keepdims=True)
        o_ref[b] = (out * lax.rsqrt(ms + RMS_EPS) * gn_ref[...]).astype(BF16)


def _s5_params(lam_re, lam_im, log_dt, b_re, b_im, c_re, c_im):
    g, p = lam_re.shape
    gpt = LANES // SSM_GROUP
    nt = g // gpt
    dt = jnp.exp(log_dt)[:, None]
    mag = jnp.exp(lam_re * dt)
    ab_re, ab_im = mag * jnp.cos(lam_im * dt), mag * jnp.sin(lam_im * dt)
    den = lam_re * lam_re + lam_im * lam_im
    nr, ni = ab_re - 1.0, ab_im
    zr = (nr * lam_re + ni * lam_im) / den
    zi = (ni * lam_re - nr * lam_im) / den
    bb_re = zr[..., None] * b_re - zi[..., None] * b_im
    bb_im = zr[..., None] * b_im + zi[..., None] * b_re
    eye = jnp.eye(gpt, dtype=F32)

    def in_map(bb):
        bb = bb.reshape(nt, gpt, p, SSM_GROUP)
        m = jnp.einsum("lgpc,gh->lgchp", bb, eye)
        return m.reshape(nt, LANES, gpt * p).astype(BF16)

    def out_map(cc):
        cc = cc.reshape(nt, gpt, SSM_GROUP, p)
        m = jnp.einsum("lgcp,gh->lhpgc", cc, eye)
        return m.reshape(nt, gpt * p, LANES).astype(BF16)

    return (in_map(bb_re), in_map(bb_im), ab_re.reshape(1, g * p), ab_im.reshape(1, g * p),
            out_map(c_re), out_map(c_im))


def _s5_mixer(proj_a, params, d_skip, glu_w, glu_b, gain, bsz, tp):
    tb = SEQ_BLOCK
    b_re, b_im, a_re, a_im, c_re, c_im = params
    nt, _, sw = b_re.shape
    width = nt * LANES
    pa = proj_a.reshape(bsz, tp, proj_a.shape[-1])
    const = lambda shp: _resident(shp, lambda t: (0,) * len(shp))
    return pl.pallas_call(
        _s5_kernel, grid=(tp // tb,),
        in_specs=[pl.BlockSpec((bsz, tb, width), lambda t: (0, t, 0)),
                  const(b_re.shape), const(b_im.shape), const(a_re.shape), const(a_im.shape),
                  const(c_re.shape), const(c_im.shape), const((1, width)), const((width, width)),
                  const((1, width)), const((1, width))],
        out_specs=pl.BlockSpec((bsz, tb, width), lambda t: (0, t, 0)),
        out_shape=jax.ShapeDtypeStruct((bsz, tp, width), BF16),
        scratch_shapes=[pltpu.VMEM((bsz, tb, nt * sw), F32), pltpu.VMEM((bsz, tb, nt * sw), F32),
                        pltpu.VMEM((bsz, SUBLANES, nt * sw), F32), pltpu.VMEM((bsz, SUBLANES, nt * sw), F32)],
        compiler_params=_cparams(("arbitrary",)), name="s5_mixer",
    )(pa, b_re, b_im, a_re, a_im, c_re, c_im, d_skip.reshape(1, width), glu_w.astype(BF16),
      glu_b.reshape(1, width), gain.reshape(1, width))


def _out_proj_kernel(alpha, ys_ref, yf_ref, yd_ref, w_ref, h_ref, g_ref, b_ref, o32_ref, o16_ref):
    n1 = ys_ref.shape[1]
    n2 = n1 + yf_ref.shape[1]
    m = (jnp.dot(ys_ref[...], w_ref[0:n1, :], preferred_element_type=F32)
         + jnp.dot(yf_ref[...], w_ref[n1:n2, :], preferred_element_type=F32)
         + jnp.dot(yd_ref[...], w_ref[n2:, :], preferred_element_type=F32))
    y = _ln_rows(alpha * h_ref[...] + m, g_ref[...], b_ref[...])
    o32_ref[...] = y
    o16_ref[...] = y.astype(BF16)


def _out_proj(y_ssm, y_fox, y_dsa, w_out, h, g, b, alpha):
    m, d = h.shape
    tm = _pick(m, (512, 256, 128))
    row = lambda w: pl.BlockSpec((tm, w), lambda i: (i, 0))
    vec = pl.BlockSpec((1, d), lambda i: (0, 0))
    return pl.pallas_call(
        functools.partial(_out_proj_kernel, alpha), grid=(m // tm,),
        in_specs=[row(y_ssm.shape[1]), row(y_fox.shape[1]), row(y_dsa.shape[1]),
                  _resident((d, d), lambda i: (0, 0)), row(d), vec, vec],
        out_specs=(row(d), row(d)),
        out_shape=(jax.ShapeDtypeStruct((m, d), F32), jax.ShapeDtypeStruct((m, d), BF16)),
        compiler_params=_cparams(("parallel",)), name="out_proj",
    )(y_ssm, y_fox, y_dsa, w_out.astype(BF16), h, g.reshape(1, d), b.reshape(1, d))


HALO = 16


def _ffn_up_kernel(h_ref, halo_ref, wv_ref, wg_ref, cwv_ref, cwg_ref, cbv_ref, cbg_ref, o_ref):
    ti = pl.program_id(1)
    h = h_ref[...]
    halo = halo_ref[...]
    first = ti == 0

    tf = wv_ref.shape[1]
    parts = 2 if tf % (2 * LANES) == 0 else 1
    pw = tf // parts

    def project(w_ref, cols):
        a = jnp.dot(h, w_ref[:, cols], preferred_element_type=F32)
        ah = jnp.dot(halo, w_ref[:, cols], preferred_element_type=F32)
        return a, ah

    def conv(proj, cw_ref, cb_ref, cols):
        a, ah = proj
        row = lax.broadcasted_iota(I32, a.shape, 0)
        ah = jnp.where(first, 0.0, ah)
        p1 = jnp.where(row == 0, ah[HALO - 1:HALO, :], pltpu.roll(a, 1, axis=0))
        p2 = jnp.where(row == 0, ah[HALO - 2:HALO - 1, :],
                       jnp.where(row == 1, ah[HALO - 1:HALO, :], pltpu.roll(a, 2, axis=0)))
        cw = cw_ref[:, cols]
        return cw[0:1, :] * p2 + cw[1:2, :] * p1 + cw[2:3, :] * a + cb_ref[:, cols]

    col = [slice(p * pw, (p + 1) * pw) for p in range(parts)]
    proj_g = [project(wg_ref, c) for c in col]
    proj_v = [project(wv_ref, c) for c in col]
    gates = [jax.nn.silu(conv(pg, cwg_ref, cbg_ref, c)) for pg, c in zip(proj_g, col)]
    for pv, gate, c in zip(proj_v, gates, col):
        o_ref[:, c] = (gate * conv(pv, cwv_ref, cbv_ref, c)).astype(BF16)


def _ffn_down_kernel(alpha, a_ref, w_ref, h_ref, g_ref, b_ref, o32_ref, o16_ref):
    f = jnp.dot(a_ref[...], w_ref[...], preferred_element_type=F32)
    y = _ln_rows(alpha * h_ref[...] + f, g_ref[...], b_ref[...])
    o32_ref[...] = y
    o16_ref[...] = y.astype(BF16)


def _conv_ffn(h16, h32, w_up, conv_w, conv_b, w_down, g, b, alpha, bsz, tp):
    d = h16.shape[-1]
    dff = w_down.shape[0]
    tm = _pick(tp, (1056, 768, 512, 256))
    tf = _pick(dff, (512, 256, 128))
    nf = dff // tf
    hb = h16.reshape(bsz, tp, d)
    w_up16 = w_up.astype(BF16)
    cw8 = jnp.zeros((SUBLANES, 2 * dff), F32).at[:CONV_WIDTH].set(conv_w)
    cb = conv_b.reshape(1, 2 * dff)
    halo_blocks = tm // HALO
    act = pl.pallas_call(
        _ffn_up_kernel, grid=(bsz, tp // tm, nf),
        in_specs=[pl.BlockSpec((None, tm, d), lambda b, i, j: (b, i, 0)),
                  pl.BlockSpec((None, HALO, d), lambda b, i, j: (b, jnp.maximum(i * halo_blocks - 1, 0), 0)),
                  pl.BlockSpec((d, tf), lambda b, i, j: (0, j)),
                  pl.BlockSpec((d, tf), lambda b, i, j: (0, j + nf)),
                  pl.BlockSpec((SUBLANES, tf), lambda b, i, j: (0, j)),
                  pl.BlockSpec((SUBLANES, tf), lambda b, i, j: (0, j + nf)),
                  pl.BlockSpec((1, tf), lambda b, i, j: (0, j)),
                  pl.BlockSpec((1, tf), lambda b, i, j: (0, j + nf))],
        out_specs=pl.BlockSpec((None, tm, tf), lambda b, i, j: (b, i, j)),
        out_shape=jax.ShapeDtypeStruct((bsz, tp, dff), BF16),
        compiler_params=_cparams(("parallel", "parallel", "parallel")), name="ffn_up_conv",
    )(hb, hb, w_up16, w_up16, cw8, cw8, cb, cb)

    m = bsz * tp
    tr = _pick(m, (384, 256, 128))
    row = lambda w: pl.BlockSpec((tr, w), lambda i: (i, 0))
    vec = pl.BlockSpec((1, d), lambda i: (0, 0))
    return pl.pallas_call(
        functools.partial(_ffn_down_kernel, alpha), grid=(m // tr,),
        in_specs=[row(dff), _resident((dff, d), lambda i: (0, 0)), row(d), vec, vec],
        out_specs=(row(d), row(d)),
        out_shape=(jax.ShapeDtypeStruct((m, d), F32), jax.ShapeDtypeStruct((m, d), BF16)),
        compiler_params=_cparams(("parallel",)), name="ffn_down_norm",
    )(act.reshape(m, dff), w_down.astype(BF16), h32, g.reshape(1, d), b.reshape(1, d))


def _split_w_in(w_in, d_model):
    ssm_w = d_model // 2
    fox_w = d_model // 4
    dsa_w = d_model - ssm_w - fox_w
    kv_rank = d_model // 16
    fox_h = fox_w // HEAD_DIM
    sizes = (ssm_w, fox_w, fox_w, fox_w, fox_h, dsa_w, kv_rank, IDX_HEADS * IDX_DIM, IDX_DIM, IDX_HEADS)
    offs = np.cumsum((0,) + sizes)
    seg = lambda k: w_in[:, offs[k]:offs[k + 1]]
    u, fq, fk, fv, ff, dq, ckv, iq, ik, iw = (seg(k) for k in range(10))
    w_a = jnp.concatenate([u, fq, fk, fv, dq, iq, ckv], axis=1).astype(BF16)
    pad = jnp.zeros((w_in.shape[0], LANES - IDX_DIM - fox_h - IDX_HEADS), w_in.dtype)
    w_b = jnp.concatenate([ik, ff, iw, pad], axis=1).astype(BF16)
    return w_a, w_b


def _mixer(h16, h32, lp, cos_t, sin_t, topk, bsz, tp, alpha):
    d = h32.shape[-1]
    w_a, w_b = lp["w_in_split"]
    proj_a = _matmul(h16, w_a, BF16)
    proj_b = _matmul(h16, w_b, F32)
    q_aug, k_aug, fv_t, q_pad, k_pair, dv_t, iq_pad, ik2, iw_t = _prep(
        proj_a, proj_b, cos_t, sin_t, lp["kv_norm_g"], lp["w_kv_up"], lp["fox_f_bias"], bsz, tp)
    ssm_w = d // 2
    fox_w = d // 4
    g_ssm, g_fox, g_dsa = jnp.split(lp["mix_norm_g"], (ssm_w, ssm_w + fox_w))
    gain = lambda g: jnp.broadcast_to(g[:, None], (g.shape[0], SEQ_BLOCK))

    y_ssm = _s5_mixer(proj_a, lp["s5_maps"], lp["ssm_d"], lp["ssm_glu_w"], lp["ssm_glu_b"], g_ssm, bsz, tp)
    y_fox = _fox_attention(q_aug, k_aug, fv_t, gain(g_fox), bsz, tp)
    y_dsa = _dsa_attention(iq_pad, iw_t, q_pad, ik2, k_pair, dv_t, gain(g_dsa), topk, bsz, tp)
    return _out_proj(y_ssm.reshape(bsz * tp, -1), y_fox.reshape(bsz * tp, -1), y_dsa.reshape(bsz * tp, -1),
                     lp["w_out"], h32, lp["ln1_g"], lp["ln1_b"], alpha)


def kernel(x, meta_tokens, ln_in_g, ln_in_b, w_in, ssm_lam_re, ssm_lam_im, ssm_log_dt, ssm_b_re, ssm_b_im,
           ssm_c_re, ssm_c_im, ssm_d, ssm_glu_w, ssm_glu_b, fox_f_bias, kv_norm_g, w_kv_up, mix_norm_g, w_out,
           ln1_g, ln1_b, w_up, conv_w, conv_b, w_down, ln2_g, ln2_b):
    bsz, seq, d = x.shape
    depth = w_in.shape[0]
    alpha = (2.0 * depth) ** 0.25
    topk = min(TOPK_MAX, seq // 4)
    t_real = seq + N_META
    tp = -(-t_real // SEQ_BLOCK) * SEQ_BLOCK

    meta = jnp.broadcast_to(meta_tokens[None].astype(x.dtype), (bsz, N_META, d))
    xcat = jnp.concatenate([meta, x, jnp.zeros((bsz, tp - t_real, d), x.dtype)], axis=1)
    h32, h16 = _layer_norm(xcat.reshape(bsz * tp, d), ln_in_g, ln_in_b)
    cos_t, sin_t = _rope_tables(tp)

    stacked = dict(w_in_split=jax.vmap(lambda w: _split_w_in(w, d))(w_in),
                   s5_maps=jax.vmap(_s5_params)(ssm_lam_re, ssm_lam_im, ssm_log_dt, ssm_b_re, ssm_b_im, ssm_c_re,
                                                ssm_c_im),
                   ssm_d=ssm_d, ssm_glu_w=ssm_glu_w, ssm_glu_b=ssm_glu_b, fox_f_bias=fox_f_bias,
                   kv_norm_g=kv_norm_g, w_kv_up=w_kv_up, mix_norm_g=mix_norm_g, w_out=w_out, ln1_g=ln1_g,
                   ln1_b=ln1_b, w_up=w_up, conv_w=conv_w, conv_b=conv_b, w_down=w_down, ln2_g=ln2_g, ln2_b=ln2_b)
    for l in range(depth):
        lp = jax.tree.map(lambda v: v[l], stacked)
        h32, h16 = _mixer(h16, h32, lp, cos_t, sin_t, topk, bsz, tp, alpha)
        h32, h16 = _conv_ffn(h16, h32, lp["w_up"], lp["conv_w"], lp["conv_b"], lp["w_down"], lp["ln2_g"],
                             lp["ln2_b"], alpha, bsz, tp)
    return h32.reshape(bsz, tp, d)[:, N_META:t_real, :]
```
